```python
import jax, jax.numpy as jnp
from jax import lax
import numpy as np

D_MODEL = 1024
BATCH = 4
SEQ = 8192
DEPTH = 2

CHUNK = 64
N_MIXERS = 2
RMS_EPS = 1e-6

MLSTM_INNER = 2 * D_MODEL
MLSTM_HEADS = 4
MLSTM_DV = MLSTM_INNER // MLSTM_HEADS
MLSTM_DQK = MLSTM_DV // 2
CONV_K = 4

ATTN_INNER = D_MODEL
ATTN_HEADS = 16
ATTN_HD = ATTN_INNER // ATTN_HEADS
LEFT_CHUNKS = 8
BAND = (LEFT_CHUNKS + 1) * CHUNK
REL_PAST = 256
REL_FUTURE = CHUNK - 1
REL_SIZE = REL_PAST + REL_FUTURE + 1

N_A = (DEPTH + 1) // 2
N_B = DEPTH // 2

kernel_name = "hybrid_mlstm_chunked_relattn"


def rms_norm(x, g):
    xf = x.astype(jnp.float32)
    y = xf * lax.rsqrt(jnp.mean(xf * xf, axis=-1, keepdims=True) + RMS_EPS)
    return (y * g.astype(jnp.float32)).astype(x.dtype)


def causal_dwconv(u, w, b):
    T = u.shape[1]
    up = jnp.pad(u, ((0, 0), (CONV_K - 1, 0), (0, 0)))
    out = b + w[0] * up[:, 0:T]
    for j in range(1, CONV_K):
        out = out + w[j] * up[:, j:j + T]
    return out


def mlstm_chunk_step(carry, xs):
    C, n, m = carry
    q, k, v, li, lf = xs
    L = q.shape[2]
    tril = jnp.tril(jnp.ones((L, L), dtype=bool))
    b = jnp.cumsum(lf, axis=-1)
    log_d = b[..., :, None] - b[..., None, :] + li[..., None, :]
    log_d = jnp.where(tril, log_d, -jnp.inf)
    m_inter = b + m[..., None]
    m_t = jnp.maximum(m_inter, jnp.max(log_d, axis=-1))
    s = jnp.einsum('bhtd,bhsd->bhts', q, k) * jnp.exp(log_d - m_t[..., None])
    inter = jnp.exp(m_inter - m_t)
    num = (jnp.einsum('bhts,bhsv->bhtv', s, v)
           + inter[..., None] * jnp.einsum('bhvd,bhtd->bhtv', C, q))
    den = jnp.sum(s, axis=-1) + inter * jnp.einsum('bhd,bhtd->bht', n, q)
    h = num / jnp.maximum(jnp.abs(den), jnp.exp(-m_t))[..., None]
    g = b[..., -1]
    a = g[..., None] - b + li
    m_new = jnp.maximum(g + m, jnp.max(a, axis=-1))
    w = jnp.exp(a - m_new[..., None])
    decay = jnp.exp(g + m - m_new)
    C_new = decay[..., None, None] * C + jnp.einsum('bhs,bhsv,bhsd->bhvd', w, v, k)
    n_new = decay[..., None] * n + jnp.einsum('bhs,bhsd->bhd', w, k)
    return (C_new, n_new, m_new), h


def mlstm_mixer(h, w_in, conv_w, conv_b, w_q, w_k, w_v, w_o, w_if, b_if, out_g, skip, w_out):
    Bsz, T, _ = h.shape
    NC = T // CHUNK
    u, z = jnp.split(h @ w_in, 2, axis=-1)
    c = jax.nn.silu(causal_dwconv(u, conv_w, conv_b))
    ch = c.reshape(Bsz, T, MLSTM_HEADS, MLSTM_DV)
    uh = u.reshape(Bsz, T, MLSTM_HEADS, MLSTM_DV)
    q = jnp.einsum('bthd,hde->bthe', ch, w_q)
    k = jnp.einsum('bthd,hde->bthe', ch, w_k) * (MLSTM_DQK ** -0.5)
    v = jnp.einsum('bthd,hde->bthe', uh, w_v)
    o = jax.nn.sigmoid(jnp.einsum('bthd,hde->bthe', uh, w_o))
    gates = (c @ w_if + b_if).astype(jnp.float32)
    li = gates[..., :MLSTM_HEADS]
    lf = jax.nn.log_sigmoid(gates[..., MLSTM_HEADS:])

    def seq_chunks(a):
        return a.astype(jnp.float32).reshape(Bsz, NC, CHUNK, MLSTM_HEADS, -1).transpose(1, 0, 3, 2, 4)

    def gate_chunks(a):
        return a.reshape(Bsz, NC, CHUNK, MLSTM_HEADS).transpose(1, 0, 3, 2)

    init = (jnp.zeros((Bsz, MLSTM_HEADS, MLSTM_DV, MLSTM_DQK), jnp.float32),
            jnp.zeros((Bsz, MLSTM_HEADS, MLSTM_DQK), jnp.float32),
            jnp.zeros((Bsz, MLSTM_HEADS), jnp.float32))
    _, hs = lax.scan(mlstm_chunk_step, init,
                     (seq_chunks(q), seq_chunks(k), seq_chunks(v), gate_chunks(li), gate_chunks(lf)))
    ht = hs.transpose(1, 0, 3, 2, 4).reshape(Bsz, T, MLSTM_HEADS, MLSTM_DV).astype(h.dtype)
    y = (o * rms_norm(ht, out_g)).reshape(Bsz, T, MLSTM_INNER)
    y = (y + skip * c) * jax.nn.silu(z)
    return y @ w_out


def chunk_attn_mixer(h, w_in, q_g, k_g, rel_bias, w_out):
    Bsz, T, _ = h.shape
    NC = T // CHUNK
    q, k, v, z = jnp.split(h @ w_in, 4, axis=-1)
    q = rms_norm(q.reshape(Bsz, T, ATTN_HEADS, ATTN_HD), q_g)
    k = rms_norm(k.reshape(Bsz, T, ATTN_HEADS, ATTN_HD), k_g)
    v = v.reshape(Bsz, T, ATTN_HEADS, ATTN_HD)
    pad = LEFT_CHUNKS * CHUNK
    t_pos = jnp.arange(CHUNK)[:, None] + pad
    p_pos = jnp.arange(BAND)[None, :]
    rel_idx = jnp.clip(t_pos - p_pos, -REL_FUTURE, REL_PAST) + REL_FUTURE
    bias = rel_bias[:, rel_idx].astype(jnp.float32)
    kp = jnp.pad(k, ((0, 0), (pad, 0), (0, 0), (0, 0)))
    vp = jnp.pad(v, ((0, 0), (pad, 0), (0, 0), (0, 0)))
    qc = q.reshape(Bsz, NC, CHUNK, ATTN_HEADS, ATTN_HD).transpose(1, 0, 2, 3, 4)
    scale = ATTN_HD ** -0.5

    def one_chunk(args):
        ci, qb = args
        start = ci * CHUNK
        kb = lax.dynamic_slice_in_dim(kp, start, BAND, axis=1)
        vb = lax.dynamic_slice_in_dim(vp, start, BAND, axis=1)
        s = jnp.einsum('blhd,bphd->bhlp', qb, kb).astype(jnp.float32) * scale + bias
        valid = (start - pad + jnp.arange(BAND)) >= 0
        s = jnp.where(valid, s, -jnp.inf)
        pr = jax.nn.softmax(s, axis=-1).astype(vb.dtype)
        return jnp.einsum('bhlp,bphd->blhd', pr, vb)

    oc = lax.map(one_chunk, (jnp.arange(NC), qc))
    o = oc.transpose(1, 0, 2, 3, 4).reshape(Bsz, T, ATTN_INNER)
    return (o * jax.nn.silu(z)) @ w_out


def setup_inputs(seed: int = 0) -> dict:
    key = jax.random.key(seed)
    ks = jax.random.split(key, 20)
    f32 = jnp.float32
    nrm = lambda k, s: jax.random.normal(k, s, f32)
    E = MLSTM_INNER
    b_i = 0.1 * nrm(ks[10], (N_A, MLSTM_HEADS))
    b_f = 3.0 + 3.0 * jax.random.uniform(ks[11], (N_A, MLSTM_HEADS), f32)
    return {
        "x": nrm(ks[0], (BATCH, SEQ, D_MODEL)),
        "norm_g": 1.0 + 0.02 * nrm(ks[1], (DEPTH, D_MODEL)),
        "a_w_in": nrm(ks[2], (N_A, D_MODEL, 2 * E)) * D_MODEL ** -0.5,
        "a_conv_w": nrm(ks[3], (N_A, CONV_K, E)) * CONV_K ** -0.5,
        "a_conv_b": 0.02 * nrm(ks[4], (N_A, E)),
        "a_w_q": nrm(ks[5], (N_A, MLSTM_HEADS, MLSTM_DV, MLSTM_DQK)) * MLSTM_DV ** -0.5,
        "a_w_k": nrm(ks[6], (N_A, MLSTM_HEADS, MLSTM_DV, MLSTM_DQK)) * MLSTM_DV ** -0.5,
        "a_w_v": nrm(ks[7], (N_A, MLSTM_HEADS, MLSTM_DV, MLSTM_DV)) * MLSTM_DV ** -0.5,
        "a_w_o": nrm(ks[8], (N_A, MLSTM_HEADS, MLSTM_DV, MLSTM_DV)) * MLSTM_DV ** -0.5,
        "a_w_if": nrm(ks[9], (N_A, E, 2 * MLSTM_HEADS)) * E ** -0.5,
        "a_b_if": jnp.concatenate([b_i, b_f], axis=-1),
        "a_out_g": 1.0 + 0.02 * nrm(ks[12], (N_A, MLSTM_HEADS, MLSTM_DV)),
        "a_skip": 1.0 + 0.02 * nrm(ks[13], (N_A, E)),
        "a_w_out": nrm(ks[14], (N_A, E, D_MODEL)) * E ** -0.5,
        "b_w_in": nrm(ks[15], (N_B, D_MODEL, 4 * ATTN_INNER)) * D_MODEL ** -0.5,
        "b_q_g": 1.0 + 0.02 * nrm(ks[16], (N_B, ATTN_HD)),
        "b_k_g": 1.0 + 0.02 * nrm(ks[17], (N_B, ATTN_HD)),
        "b_rel_bias": 0.1 * nrm(ks[18], (N_B, ATTN_HEADS, REL_SIZE)),
        "b_w_out": nrm(ks[19], (N_B, ATTN_INNER, D_MODEL)) * ATTN_INNER ** -0.5,
    }


def reference(x, norm_g, a_w_in, a_conv_w, a_conv_b, a_w_q, a_w_k, a_w_v, a_w_o, a_w_if, a_b_if,
              a_out_g, a_skip, a_w_out, b_w_in, b_q_g, b_k_g, b_rel_bias, b_w_out):
    for i in range(DEPTH):
        hn = rms_norm(x, norm_g[i])
        j = i // N_MIXERS
        if i % N_MIXERS == 0:
            y = mlstm_mixer(hn, a_w_in[j], a_conv_w[j], a_conv_b[j], a_w_q[j], a_w_k[j], a_w_v[j],
                            a_w_o[j], a_w_if[j], a_b_if[j], a_out_g[j], a_skip[j], a_w_out[j])
        else:
            y = chunk_attn_mixer(hn, b_w_in[j], b_q_g[j], b_k_g[j], b_rel_bias[j], b_w_out[j])
        x = x + y
    return x
```

```python
import functools

import jax
import jax.numpy as jnp
from jax import lax
from jax.experimental import pallas as pl
from jax.experimental.pallas import tpu as pltpu

F32 = jnp.float32
BF16 = jnp.bfloat16

RMS_EPS = 1e-6
CHUNK = 64

D_MODEL = 1024
M_INNER = 2 * D_MODEL
M_HEADS = 4
M_DV = M_INNER // M_HEADS
M_DQK = M_DV // 2
CONV_K = 4

A_INNER = D_MODEL
A_HEADS = 16
A_HD = A_INNER // A_HEADS
A_PAIRS = A_HEADS // 2
LEFT_CHUNKS = 8
BAND = (LEFT_CHUNKS + 1) * CHUNK
REL_PAST = 256
REL_FUTURE = CHUNK - 1
REL_SIZE = REL_PAST + REL_FUTURE + 1

LANES = 128
SUBLANES = 8
GATE_PAD = LANES
VMEM_LIMIT = 56 * 1024 * 1024

PRE_BM = 256
REC_L = 256
APRE_BM = 256
ATT_BM = LEFT_CHUNKS * CHUNK
BIAS_W = 640

NT_DIMS = (((1,), (1,)), ((), ()))
TN_DIMS = (((0,), (0,)), ((), ()))


def _sigmoid(x):
    return 1.0 / (1.0 + jnp.exp(-x))


def _log_sigmoid(x):
    return jnp.minimum(x, 0.0) - jnp.log1p(jnp.exp(-jnp.abs(x)))


def _rms(x, eps=RMS_EPS):
    return x * lax.rsqrt(jnp.mean(x * x, axis=-1, keepdims=True) + eps)


def _mlstm_pre_kernel(x_ref, g_ref, win_ref, cw_ref, cb_ref, wq_ref, wk_ref, wv_ref, wo_ref,
                      wif_ref, bif_ref,
                      q_ref, k_ref, v_ref, o_ref, c_ref, sz_ref, gc_ref, gr_ref,
                      ubuf):
    t = pl.program_id(1)
    bm = x_ref.shape[0]

    @pl.when(t == 0)
    def _():
        ubuf[0:SUBLANES, :] = jnp.zeros((SUBLANES, M_INNER), F32)

    xn = (_rms(x_ref[...]) * g_ref[...]).astype(BF16)
    gates = jnp.zeros((bm, GATE_PAD), F32) + bif_ref[...]
    for h in range(M_HEADS):
        lo, hi = h * M_DV, (h + 1) * M_DV
        u = jnp.dot(xn, win_ref[:, lo:hi], preferred_element_type=F32)
        z = jnp.dot(xn, win_ref[:, M_INNER + lo:M_INNER + hi], preferred_element_type=F32)
        ubuf[SUBLANES:SUBLANES + bm, lo:hi] = u
        conv = cb_ref[:, lo:hi] + cw_ref[CONV_K - 1:CONV_K, lo:hi] * u
        for j in range(CONV_K - 1):
            shifted = ubuf[pl.ds(SUBLANES - (CONV_K - 1) + j, bm), lo:hi]
            conv = conv + cw_ref[j:j + 1, lo:hi] * shifted
        ubuf[0:SUBLANES, lo:hi] = u[bm - SUBLANES:bm, :]
        c = conv * _sigmoid(conv)
        cb16 = c.astype(BF16)
        ub16 = u.astype(BF16)
        q_ref[:, h * M_DQK:(h + 1) * M_DQK] = jnp.dot(
            cb16, wq_ref[h], preferred_element_type=F32).astype(BF16)
        k_ref[:, h * M_DQK:(h + 1) * M_DQK] = (jnp.dot(
            cb16, wk_ref[h], preferred_element_type=F32) * (M_DQK ** -0.5)).astype(BF16)
        v_ref[:, lo:hi] = jnp.dot(ub16, wv_ref[h], preferred_element_type=F32).astype(BF16)
        o_ref[:, lo:hi] = _sigmoid(jnp.dot(ub16, wo_ref[h], preferred_element_type=F32)).astype(BF16)
        c_ref[:, lo:hi] = cb16
        sz_ref[:, lo:hi] = (z * _sigmoid(z)).astype(BF16)
        gates = gates + jnp.dot(cb16, wif_ref[lo:hi, :], preferred_element_type=F32)
    gc_ref[...] = gates
    gr_ref[...] = gates.T[0:2 * M_HEADS, :]


def _mlstm_pre(x, g, w_in, conv_w, conv_b, w_q, w_k, w_v, w_o, w_if, b_if):
    B, T, D = x.shape
    bm = min(PRE_BM, T)
    nt = T // bm
    full = lambda *shape: pl.BlockSpec(shape, lambda b, t: (0,) * len(shape))
    tok = lambda width: pl.BlockSpec((None, bm, width), lambda b, t: (b, t, 0))
    out_shape = (
        jax.ShapeDtypeStruct((B, T, M_HEADS * M_DQK), BF16),
        jax.ShapeDtypeStruct((B, T, M_HEADS * M_DQK), BF16),
        jax.ShapeDtypeStruct((B, T, M_INNER), BF16),
        jax.ShapeDtypeStruct((B, T, M_INNER), BF16),
        jax.ShapeDtypeStruct((B, T, M_INNER), BF16),
        jax.ShapeDtypeStruct((B, T, M_INNER), BF16),
        jax.ShapeDtypeStruct((B, T, GATE_PAD), F32),
        jax.ShapeDtypeStruct((B, 2 * M_HEADS, T), F32),
    )
    return pl.pallas_call(
        _mlstm_pre_kernel,
        grid=(B, nt),
        in_specs=[
            tok(D), full(1, D), full(D, 2 * M_INNER), full(CONV_K, M_INNER), full(1, M_INNER),
            full(M_HEADS, M_DV, M_DQK), full(M_HEADS, M_DV, M_DQK),
            full(M_HEADS, M_DV, M_DV), full(M_HEADS, M_DV, M_DV),
            full(M_INNER, GATE_PAD), full(1, GATE_PAD),
        ],
        out_specs=(
            tok(M_HEADS * M_DQK), tok(M_HEADS * M_DQK), tok(M_INNER), tok(M_INNER), tok(M_INNER),
            tok(M_INNER), tok(GATE_PAD),
            pl.BlockSpec((None, 2 * M_HEADS, bm), lambda b, t: (b, 0, t)),
        ),
        out_shape=out_shape,
        scratch_shapes=[pltpu.VMEM((SUBLANES + bm, M_INNER), F32)],
        compiler_params=pltpu.CompilerParams(
            dimension_semantics=("parallel", "arbitrary"), vmem_limit_bytes=VMEM_LIMIT),
        name="mlstm_pre",
    )(x, g, w_in, conv_w, conv_b, w_q, w_k, w_v, w_o, w_if, b_if)


def _mlstm_rec_kernel(q_ref, k_ref, v_ref, o_ref, c_ref, sz_ref, x_ref, gc_ref, gr_ref,
                      og_ref, skip_ref, wout_ref,
                      out_ref,
                      ct_ref, n_ref, m_ref, y_ref):
    t = pl.program_id(1)
    L = q_ref.shape[0]

    @pl.when(t == 0)
    def _():
        ct_ref[...] = jnp.zeros(ct_ref.shape, F32)
        n_ref[...] = jnp.zeros(n_ref.shape, F32)
        m_ref[...] = jnp.zeros(m_ref.shape, F32)

    gc = gc_ref[...]
    gr = gr_ref[...]
    row = lax.broadcasted_iota(jnp.int32, (L, L), 0)
    col = lax.broadcasted_iota(jnp.int32, (L, L), 1)
    tril = col <= row
    bc_all = jnp.dot(tril.astype(F32), _log_sigmoid(gc), precision=lax.Precision.HIGHEST,
                     preferred_element_type=F32)
    br_all = jnp.dot(_log_sigmoid(gr), (row <= col).astype(F32), precision=lax.Precision.HIGHEST,
                     preferred_element_type=F32)
    m_all = m_ref[...]

    for h in range(M_HEADS):
        lo, hi = h * M_DV, (h + 1) * M_DV
        li_c = gc[:, h:h + 1]
        b_c = bc_all[:, M_HEADS + h:M_HEADS + h + 1]
        li_r = gr[h:h + 1, :]
        b_r = br_all[M_HEADS + h:M_HEADS + h + 1, :]
        m_prev = m_all[:, h:h + 1]

        log_d = jnp.where(tril, b_c - b_r + li_r, -jnp.inf)
        m_inter = b_c + m_prev
        m_t = jnp.maximum(m_inter, jnp.max(log_d, axis=1, keepdims=True))
        dmat = jnp.exp(log_d - m_t)
        inter = jnp.exp(m_inter - m_t)

        qh = q_ref[:, h * M_DQK:(h + 1) * M_DQK]
        kh = k_ref[:, h * M_DQK:(h + 1) * M_DQK]
        vh = v_ref[:, lo:hi]
        s = lax.dot_general(qh, kh, NT_DIMS, preferred_element_type=F32) * dmat
        ct_old = ct_ref[h]
        num = (jnp.dot(s.astype(BF16), vh, preferred_element_type=F32)
               + inter * jnp.dot(qh, ct_old.astype(BF16), preferred_element_type=F32))
        den = (jnp.sum(s, axis=1, keepdims=True)
               + inter * jnp.sum(qh.astype(F32) * n_ref[h], axis=1, keepdims=True))
        hh = num / jnp.maximum(jnp.abs(den), jnp.exp(-m_t))

        g_last = b_c[L - 1:L, :]
        a_c = g_last - b_c + li_c
        m_new = jnp.maximum(g_last + m_prev, jnp.max(a_c, axis=0, keepdims=True))
        w_c = jnp.exp(a_c - m_new)
        decay = jnp.exp(g_last + m_prev - m_new)
        kw = kh.astype(F32) * w_c
        ct_ref[h] = decay * ct_old + lax.dot_general(kw.astype(BF16), vh, TN_DIMS,
                                                     preferred_element_type=F32)
        n_ref[h] = decay * n_ref[h] + jnp.sum(kw, axis=0, keepdims=True)
        m_ref[:, h:h + 1] = m_new

        hn = _rms(hh) * og_ref[:, lo:hi]
        y = (o_ref[:, lo:hi].astype(F32) * hn + skip_ref[:, lo:hi] * c_ref[:, lo:hi].astype(F32))
        y_ref[:, lo:hi] = (y * sz_ref[:, lo:hi].astype(F32)).astype(BF16)

    out_ref[...] = x_ref[...] + jnp.dot(y_ref[...], wout_ref[...], preferred_element_type=F32)


def _mlstm_rec(q, k, v, o, c, sz, x, gc, gr, out_g, skip, w_out):
    B, T, D = x.shape
    L = min(REC_L, T)
    nt = T // L
    full = lambda *shape: pl.BlockSpec(shape, lambda b, t: (0,) * len(shape))
    tok = lambda width: pl.BlockSpec((None, L, width), lambda b, t: (b, t, 0))
    return pl.pallas_call(
        _mlstm_rec_kernel,
        grid=(B, nt),
        in_specs=[
            tok(M_HEADS * M_DQK), tok(M_HEADS * M_DQK), tok(M_INNER), tok(M_INNER), tok(M_INNER),
            tok(M_INNER), tok(D), tok(GATE_PAD),
            pl.BlockSpec((None, 2 * M_HEADS, L), lambda b, t: (b, 0, t)),
            full(1, M_INNER), full(1, M_INNER), full(M_INNER, D),
        ],
        out_specs=tok(D),
        out_shape=jax.ShapeDtypeStruct((B, T, D), F32),
        scratch_shapes=[
            pltpu.VMEM((M_HEADS, M_DQK, M_DV), F32),
            pltpu.VMEM((M_HEADS, 1, M_DQK), F32),
            pltpu.VMEM((1, LANES), F32),
            pltpu.VMEM((L, M_INNER), BF16),
        ],
        compiler_params=pltpu.CompilerParams(
            dimension_semantics=("parallel", "arbitrary"), vmem_limit_bytes=VMEM_LIMIT),
        name="mlstm_rec",
    )(q, k, v, o, c, sz, x, gc, gr, out_g, skip, w_out)


def _bias_kernel(tab_ref, out_ref):
    r = lax.broadcasted_iota(jnp.int32, (REL_SIZE, BIAS_W), 0)
    e = lax.broadcasted_iota(jnp.int32, (REL_SIZE, BIAS_W), 1)
    e = jnp.where(e >= BAND, e - BIAS_W, e)
    idx = jnp.clip(LEFT_CHUNKS * CHUNK - e, -REL_FUTURE, REL_PAST) + REL_FUTURE
    onehot = (r == idx).astype(F32)
    g = jnp.dot(tab_ref[...], onehot, precision=lax.Precision.HIGHEST,
                preferred_element_type=F32)
    for j in range(A_PAIRS):
        halves = []
        for a in range(2):
            rows = jnp.broadcast_to(g[2 * j + a:2 * j + a + 1, :], (CHUNK, BIAS_W))
            halves.append(pltpu.roll(rows, 0, 1, stride=1, stride_axis=0))
        both = jnp.concatenate(halves, axis=0)
        out_ref[j] = both.T[0:BAND, :]


def _attn_bias(rel_bias):
    return pl.pallas_call(
        _bias_kernel,
        out_shape=jax.ShapeDtypeStruct((A_PAIRS, BAND, 2 * CHUNK), F32),
        compiler_params=pltpu.CompilerParams(vmem_limit_bytes=VMEM_LIMIT),
        name="attn_bias",
    )(rel_bias)


def _attn_pre_kernel(x_ref, g_ref, win_ref, gain_ref, seg_ref, exp_ref,
                     q_ref, k_ref, v_ref, sz_ref):
    xn = (_rms(x_ref[...]) * g_ref[...]).astype(BF16)
    qk = jnp.dot(xn, win_ref[:, 0:2 * A_INNER], preferred_element_type=F32)
    ss = jnp.dot((qk * qk).astype(BF16), seg_ref[...], preferred_element_type=F32)
    inv = lax.rsqrt(ss * (1.0 / A_HD) + RMS_EPS)
    inv_hi = inv.astype(BF16)
    inv_lo = (inv - inv_hi.astype(F32)).astype(BF16)
    inv_e = jnp.dot(jnp.concatenate([inv_hi, inv_lo], axis=1), exp_ref[...],
                    preferred_element_type=F32)
    qkn = qk * inv_e * gain_ref[...]
    q_ref[...] = qkn[:, 0:A_INNER].astype(BF16)
    k_ref[...] = qkn[:, A_INNER:2 * A_INNER].astype(BF16)
    v_ref[...] = jnp.dot(xn, win_ref[:, 2 * A_INNER:3 * A_INNER],
                         preferred_element_type=F32).astype(BF16)
    z = jnp.dot(xn, win_ref[:, 3 * A_INNER:4 * A_INNER], preferred_element_type=F32)
    sz_ref[...] = (z * _sigmoid(z)).astype(BF16)


def _attn_pre(x, g, w_in, gain, seg, expand):
    B, T, D = x.shape
    bm = min(APRE_BM, T)
    nt = T // bm
    full = lambda *shape: pl.BlockSpec(shape, lambda b, t: (0,) * len(shape))
    tok = lambda width: pl.BlockSpec((None, bm, width), lambda b, t: (b, t, 0))
    o = jax.ShapeDtypeStruct((B, T, A_INNER), BF16)
    return pl.pallas_call(
        _attn_pre_kernel,
        grid=(B, nt),
        in_specs=[tok(D), full(1, D), full(D, 4 * A_INNER), full(1, 2 * A_INNER),
                  full(2 * A_INNER, LANES), full(2 * LANES, 2 * A_INNER)],
        out_specs=(tok(A_INNER),) * 4,
        out_shape=(o, o, o, o),
        compiler_params=pltpu.CompilerParams(
            dimension_semantics=("parallel", "parallel"), vmem_limit_bytes=VMEM_LIMIT),
        name="attn_pre",
    )(x, g, w_in, gain, seg, expand)


def _attn_kernel(q_ref, kp_ref, kc_ref, vp_ref, vc_ref, sz_ref, x_ref, bias_ref, wout_ref,
                 out_ref,
                 kcat, vcat, o_s):
    i = pl.program_id(1)
    bm = q_ref.shape[0]
    n_chunks = bm // CHUNK
    kcat[0:bm, :] = kp_ref[...]
    kcat[bm:2 * bm, :] = kc_ref[...]
    vcat[0:bm, :] = vp_ref[...]
    vcat[bm:2 * bm, :] = vc_ref[...]
    first_head = lax.broadcasted_iota(jnp.int32, (CHUNK, 2 * A_HD), 1) < A_HD
    key_off = lax.broadcasted_iota(jnp.int32, (BAND, 1), 0)

    def chunk_body(c, carry):
        r0 = pl.multiple_of(c * CHUNK, CHUNK)
        band_start = i * bm + c * CHUNK - LEFT_CHUNKS * CHUNK
        valid = (band_start + key_off) >= 0
        for j in range(A_PAIRS):
            lanes = slice(j * 2 * A_HD, (j + 1) * 2 * A_HD)
            q2 = q_ref[pl.ds(r0, CHUNK), lanes].astype(F32)
            wt = jnp.concatenate([jnp.where(first_head, q2, 0.0),
                                  jnp.where(first_head, 0.0, q2)], axis=0).astype(BF16)
            k2 = kcat[pl.ds(r0 + bm - LEFT_CHUNKS * CHUNK, BAND), lanes]
            v2 = vcat[pl.ds(r0 + bm - LEFT_CHUNKS * CHUNK, BAND), lanes]
            s = lax.dot_general(k2, wt, NT_DIMS, preferred_element_type=F32) + bias_ref[j]
            s = jnp.where(valid, s, -jnp.inf)
            e = jnp.exp(s - jnp.max(s, axis=0, keepdims=True))
            p = e * (1.0 / jnp.sum(e, axis=0, keepdims=True))
            r = lax.dot_general(p.astype(BF16), v2, TN_DIMS, preferred_element_type=F32)
            o2 = jnp.where(first_head, r[0:CHUNK, :], r[CHUNK:2 * CHUNK, :])
            o_s[pl.ds(r0, CHUNK), lanes] = (
                o2 * sz_ref[pl.ds(r0, CHUNK), lanes].astype(F32)).astype(BF16)
        return carry

    lax.fori_loop(0, n_chunks, chunk_body, 0)
    out_ref[...] = x_ref[...] + jnp.dot(o_s[...], wout_ref[...], preferred_element_type=F32)


def _attn(q, k, v, sz, x, bias, w_out):
    B, T, D = x.shape
    bm = ATT_BM
    nt = T // bm
    full = lambda *shape: pl.BlockSpec(shape, lambda b, t: (0,) * len(shape))
    cur = lambda width: pl.BlockSpec((None, bm, width), lambda b, t: (b, t, 0))
    prev = lambda width: pl.BlockSpec((None, bm, width), lambda b, t: (b, jnp.maximum(t - 1, 0), 0))
    return pl.pallas_call(
        _attn_kernel,
        grid=(B, nt),
        in_specs=[cur(A_INNER), prev(A_INNER), cur(A_INNER), prev(A_INNER), cur(A_INNER),
                  cur(A_INNER), cur(D), full(A_PAIRS, BAND, 2 * CHUNK), full(A_INNER, D)],
        out_specs=cur(D),
        out_shape=jax.ShapeDtypeStruct((B, T, D), F32),
        scratch_shapes=[
            pltpu.VMEM((2 * bm, A_INNER), BF16),
            pltpu.VMEM((2 * bm, A_INNER), BF16),
            pltpu.VMEM((bm, A_INNER), BF16),
        ],
        compiler_params=pltpu.CompilerParams(
            dimension_semantics=("parallel", "parallel"), vmem_limit_bytes=VMEM_LIMIT),
        name="attn",
    )(q, k, k, v, v, sz, x, bias, w_out)


def _qk_norm_constants():
    col = jnp.arange(2 * A_INNER) // A_HD
    seg = (col[:, None] == jnp.arange(LANES)[None, :]).astype(BF16)
    row = jnp.arange(2 * LANES) % LANES
    expand = (row[:, None] == col[None, :]).astype(BF16)
    return seg, expand


def kernel(x, norm_g, a_w_in, a_conv_w, a_conv_b, a_w_q, a_w_k, a_w_v, a_w_o, a_w_if, a_b_if,
           a_out_g, a_skip, a_w_out, b_w_in, b_q_g, b_k_g, b_rel_bias, b_w_out):
    assert x.shape[1] % ATT_BM == 0 and x.shape[2] == D_MODEL
    bf = lambda w: w.astype(BF16)
    pad_gate = lambda w: jnp.pad(w, ((0, 0), (0, GATE_PAD - 2 * M_HEADS)))

    q, k, v, o, c, sz, gc, gr = _mlstm_pre(
        x, norm_g[0][None, :], bf(a_w_in[0]), a_conv_w[0], a_conv_b[0][None, :],
        bf(a_w_q[0]), bf(a_w_k[0]), bf(a_w_v[0]), bf(a_w_o[0]),
        bf(pad_gate(a_w_if[0])), pad_gate(a_b_if[0][None, :]))
    x1 = _mlstm_rec(q, k, v, o, c, sz, x, gc, gr,
                    a_out_g[0].reshape(1, M_INNER), a_skip[0][None, :], bf(a_w_out[0]))

    bias = _attn_bias(b_rel_bias[0])
    gain = jnp.concatenate([jnp.tile(b_q_g[0], A_HEADS) * (A_HD ** -0.5),
                            jnp.tile(b_k_g[0], A_HEADS)])[None, :]
    seg, expand = _qk_norm_constants()
    q, k, v, sz = _attn_pre(x1, norm_g[1][None, :], bf(b_w_in[0]), gain, seg, expand)
    return _attn(q, k, v, sz, x1, bias, bf(b_w_out[0]))
```

```python
import functools

import jax
import jax.numpy as jnp
from jax import lax
from jax.experimental import pallas as pl
from jax.experimental.pallas import tpu as pltpu

F32 = jnp.float32
BF16 = jnp.bfloat16

RMS_EPS = 1e-6
CHUNK = 64

D_MODEL = 1024
M_INNER = 2 * D_MODEL
M_HEADS = 4
M_DV = M_INNER // M_HEADS
M_DQK = M_DV // 2
CONV_K = 4

A_INNER = D_MODEL
A_HEADS = 16
A_HD = A_INNER // A_HEADS
A_PAIRS = A_HEADS // 2
LEFT_CHUNKS = 8
BAND = (LEFT_CHUNKS + 1) * CHUNK
REL_PAST = 256
REL_FUTURE = CHUNK - 1
REL_SIZE = REL_PAST + REL_FUTURE + 1

LANES = 128
SUBLANES = 8
GATE_PAD = LANES
VMEM_LIMIT = 56 * 1024 * 1024

PRE_BM = 256
REC_L = 256
APRE_BM = 256
ATT_BM = LEFT_CHUNKS * CHUNK
BIAS_W = 640

NT_DIMS = (((1,), (1,)), ((), ()))
TN_DIMS = (((0,), (0,)), ((), ()))


def _sigmoid(x):
    return 1.0 / (1.0 + jnp.exp(-x))


def _log_sigmoid(x):
    return jnp.minimum(x, 0.0) - jnp.log1p(jnp.exp(-jnp.abs(x)))


def _rms(x, eps=RMS_EPS):
    return x * lax.rsqrt(jnp.mean(x * x, axis=-1, keepdims=True) + eps)


def _mlstm_pre_kernel(x_ref, g_ref, win_ref, cw_ref, cb_ref, wq_ref, wk_ref, wv_ref, wo_ref,
                      wif_ref, bif_ref,
                      q_ref, k_ref, v_ref, o_ref, c_ref, sz_ref, gc_ref, gr_ref,
                      ubuf):
    t = pl.program_id(1)
    bm = x_ref.shape[0]

    @pl.when(t == 0)
    def _():
        ubuf[0:SUBLANES, :] = jnp.zeros((SUBLANES, M_INNER), F32)

    xn = (_rms(x_ref[...]) * g_ref[...]).astype(BF16)
    gates = jnp.zeros((bm, GATE_PAD), F32) + bif_ref[...]
    for h in range(M_HEADS):
        lo, hi = h * M_DV, (h + 1) * M_DV
        u = jnp.dot(xn, win_ref[:, lo:hi], preferred_element_type=F32)
        z = jnp.dot(xn, win_ref[:, M_INNER + lo:M_INNER + hi], preferred_element_type=F32)
        ubuf[SUBLANES:SUBLANES + bm, lo:hi] = u
        conv = cb_ref[:, lo:hi] + cw_ref[CONV_K - 1:CONV_K, lo:hi] * u
        for j in range(CONV_K - 1):
            shifted = ubuf[pl.ds(SUBLANES - (CONV_K - 1) + j, bm), lo:hi]
            conv = conv + cw_ref[j:j + 1, lo:hi] * shifted
        ubuf[0:SUBLANES, lo:hi] = u[bm - SUBLANES:bm, :]
        c = conv * _sigmoid(conv)
        cb16 = c.astype(BF16)
        ub16 = u.astype(BF16)
        q_ref[:, h * M_DQK:(h + 1) * M_DQK] = jnp.dot(
            cb16, wq_ref[h], preferred_element_type=F32).astype(BF16)
        k_ref[:, h * M_DQK:(h + 1) * M_DQK] = (jnp.dot(
            cb16, wk_ref[h], preferred_element_type=F32) * (M_DQK ** -0.5)).astype(BF16)
        v_ref[:, lo:hi] = jnp.dot(ub16, wv_ref[h], preferred_element_type=F32).astype(BF16)
        o_ref[:, lo:hi] = _sigmoid(jnp.dot(ub16, wo_ref[h], preferred_element_type=F32)).astype(BF16)
        c_ref[:, lo:hi] = cb16
        sz_ref[:, lo:hi] = (z * _sigmoid(z)).astype(BF16)
        gates = gates + jnp.dot(cb16, wif_ref[lo:hi, :], preferred_element_type=F32)
    gc_ref[...] = gates
    gr_ref[...] = gates.T[0:2 * M_HEADS, :]


def _mlstm_pre(x, g, w_in, conv_w, conv_b, w_q, w_k, w_v, w_o, w_if, b_if):
    B, T, D = x.shape
    bm = min(PRE_BM, T)
    nt = T // bm
    full = lambda *shape: pl.BlockSpec(shape, lambda b, t: (0,) * len(shape))
    tok = lambda width: pl.BlockSpec((None, bm, width), lambda b, t: (b, t, 0))
    out_shape = (
        jax.ShapeDtypeStruct((B, T, M_HEADS * M_DQK), BF16),
        jax.ShapeDtypeStruct((B, T, M_HEADS * M_DQK), BF16),
        jax.ShapeDtypeStruct((B, T, M_INNER), BF16),
        jax.ShapeDtypeStruct((B, T, M_INNER), BF16),
        jax.ShapeDtypeStruct((B, T, M_INNER), BF16),
        jax.ShapeDtypeStruct((B, T, M_INNER), BF16),
        jax.ShapeDtypeStruct((B, T, GATE_PAD), F32),
        jax.ShapeDtypeStruct((B, 2 * M_HEADS, T), F32),
    )
    return pl.pallas_call(
        _mlstm_pre_kernel,
        grid=(B, nt),
        in_specs=[
            tok(D), full(1, D), full(D, 2 * M_INNER), full(CONV_K, M_INNER), full(1, M_INNER),
            full(M_HEADS, M_DV, M_DQK), full(M_HEADS, M_DV, M_DQK),
            full(M_HEADS, M_DV, M_DV), full(M_HEADS, M_DV, M_DV),
            full(M_INNER, GATE_PAD), full(1, GATE_PAD),
        ],
        out_specs=(
            tok(M_HEADS * M_DQK), tok(M_HEADS * M_DQK), tok(M_INNER), tok(M_INNER), tok(M_INNER),
            tok(M_INNER), tok(GATE_PAD),
            pl.BlockSpec((None, 2 * M_HEADS, bm), lambda b, t: (b, 0, t)),
        ),
        out_shape=out_shape,
        scratch_shapes=[pltpu.VMEM((SUBLANES + bm, M_INNER), F32)],
        compiler_params=pltpu.CompilerParams(
            dimension_semantics=("parallel", "arbitrary"), vmem_limit_bytes=VMEM_LIMIT),
        name="mlstm_pre",
    )(x, g, w_in, conv_w, conv_b, w_q, w_k, w_v, w_o, w_if, b_if)


def _mlstm_rec_kernel(q_ref, k_ref, v_ref, o_ref, c_ref, sz_ref, x_ref, gc_ref, gr_ref,
                      og_ref, skip_ref, wout_ref,
                      out_ref,
                      ct_ref, n_ref, m_ref, y_ref):
    t = pl.program_id(1)
    L = q_ref.shape[0]

    @pl.when(t == 0)
    def _():
        ct_ref[...] = jnp.zeros(ct_ref.shape, F32)
        n_ref[...] = jnp.zeros(n_ref.shape, F32)
        m_ref[...] = jnp.zeros(m_ref.shape, F32)

    gc = gc_ref[...]
    gr = gr_ref[...]
    row = lax.broadcasted_iota(jnp.int32, (L, L), 0)
    col = lax.broadcasted_iota(jnp.int32, (L, L), 1)
    tril = col <= row
    bc_all = jnp.dot(tril.astype(F32), _log_sigmoid(gc), precision=lax.Precision.HIGHEST,
                     preferred_element_type=F32)
    br_all = jnp.dot(_log_sigmoid(gr), (row <= col).astype(F32), precision=lax.Precision.HIGHEST,
                     preferred_element_type=F32)
    m_all = m_ref[...]

    for h in range(M_HEADS):
        lo, hi = h * M_DV, (h + 1) * M_DV
        li_c = gc[:, h:h + 1]
        b_c = bc_all[:, M_HEADS + h:M_HEADS + h + 1]
        li_r = gr[h:h + 1, :]
        b_r = br_all[M_HEADS + h:M_HEADS + h + 1, :]
        m_prev = m_all[:, h:h + 1]

        log_d = jnp.where(tril, b_c - b_r + li_r, -jnp.inf)
        m_inter = b_c + m_prev
        m_t = jnp.maximum(m_inter, jnp.max(log_d, axis=1, keepdims=True))
        dmat = jnp.exp(log_d - m_t)
        inter = jnp.exp(m_inter - m_t)

        qh = q_ref[:, h * M_DQK:(h + 1) * M_DQK]
        kh = k_ref[:, h * M_DQK:(h + 1) * M_DQK]
        vh = v_ref[:, lo:hi]
        s = lax.dot_general(qh, kh, NT_DIMS, preferred_element_type=F32) * dmat
        ct_old = ct_ref[h]
        num = (jnp.dot(s.astype(BF16), vh, preferred_element_type=F32)
               + inter * jnp.dot(qh, ct_old.astype(BF16), preferred_element_type=F32))
        den = (jnp.sum(s, axis=1, keepdims=True)
               + inter * jnp.sum(qh.astype(F32) * n_ref[h], axis=1, keepdims=True))
        hh = num / jnp.maximum(jnp.abs(den), jnp.exp(-m_t))

        g_last = b_c[L - 1:L, :]
        a_c = g_last - b_c + li_c
        m_new = jnp.maximum(g_last + m_prev, jnp.max(a_c, axis=0, keepdims=True))
        w_c = jnp.exp(a_c - m_new)
        decay = jnp.exp(g_last + m_prev - m_new)
        kw = kh.astype(F32) * w_c
        ct_ref[h] = decay * ct_old + lax.dot_general(kw.astype(BF16), vh, TN_DIMS,
                                                     preferred_element_type=F32)
        n_ref[h] = decay * n_ref[h] + jnp.sum(kw, axis=0, keepdims=True)
        m_ref[:, h:h + 1] = m_new

        hn = _rms(hh) * og_ref[:, lo:hi]
        y = (o_ref[:, lo:hi].astype(F32) * hn + skip_ref[:, lo:hi] * c_ref[:, lo:hi].astype(F32))
        y_ref[:, lo:hi] = (y * sz_ref[:, lo:hi].astype(F32)).astype(BF16)

    out_ref[...] = x_ref[...] + jnp.dot(y_ref[...], wout_ref[...], preferred_element_type=F32)


def _mlstm_rec(q, k, v, o, c, sz, x, gc, gr, out_g, skip, w_out):
    B, T, D = x.shape
    L = min(REC_L, T)
    nt = T // L
    full = lambda *shape: pl.BlockSpec(shape, lambda b, t: (0,) * len(shape))
    tok = lambda width: pl.BlockSpec((None, L, width), lambda b, t: (b, t, 0))
    return pl.pallas_call(
        _mlstm_rec_kernel,
        grid=(B, nt),
        in_specs=[
            tok(M_HEADS * M_DQK), tok(M_HEADS * M_DQK), tok(M_INNER), tok(M_INNER), tok(M_INNER),
            tok(M_INNER), tok(D), tok(GATE_PAD),
            pl.BlockSpec((None, 2 * M_HEADS, L), lambda b, t: (b, 0, t)),
            full(1, M_INNER), full(1, M_INNER), full(M_INNER, D),
        ],
        out_specs=tok(D),
        out_shape=jax.ShapeDtypeStruct((B, T, D), F32),
        scratch_shapes=[
            pltpu.VMEM((M_HEADS, M_DQK, M_DV), F32),
            pltpu.VMEM((M_HEADS, 1, M_DQK), F32),
            pltpu.VMEM((1, LANES), F32),
            pltpu.VMEM((L, M_INNER), BF16),
        ],
        compiler_params=pltpu.CompilerParams(
            dimension_semantics=("parallel", "arbitrary"), vmem_limit_bytes=VMEM_LIMIT),
        name="mlstm_rec",
    )(q, k, v, o, c, sz, x, gc, gr, out_g, skip, w_out)


def _bias_kernel(tab_ref, out_ref):
    r = lax.broadcasted_iota(jnp.int32, (REL_SIZE, BIAS_W), 0)
    e = lax.broadcasted_iota(jnp.int32, (REL_SIZE, BIAS_W), 1)
    e = jnp.where(e >= BAND, e - BIAS_W, e)
    idx = jnp.clip(LEFT_CHUNKS * CHUNK - e, -REL_FUTURE, REL_PAST) + REL_FUTURE
    onehot = (r == idx).astype(F32)
    g = jnp.dot(tab_ref[...], onehot, precision=lax.Precision.HIGHEST,
                preferred_element_type=F32)
    for j in range(A_PAIRS):
        halves = []
        for a in range(2):
            rows = jnp.broadcast_to(g[2 * j + a:2 * j + a + 1, :], (CHUNK, BIAS_W))
            halves.append(pltpu.roll(rows, 0, 1, stride=1, stride_axis=0))
        both = jnp.concatenate(halves, axis=0)
        out_ref[j] = both[:, 0:BAND]


def _attn_bias(rel_bias):
    return pl.pallas_call(
        _bias_kernel,
        out_shape=jax.ShapeDtypeStruct((A_PAIRS, 2 * CHUNK, BAND), F32),
        compiler_params=pltpu.CompilerParams(vmem_limit_bytes=VMEM_LIMIT),
        name="attn_bias",
    )(rel_bias)


def _attn_pre_kernel(x_ref, g_ref, win_ref, gain_ref, seg_ref, exp_ref,
                     q_ref, k_ref, v_ref, sz_ref):
    xn = (_rms(x_ref[...]) * g_ref[...]).astype(BF16)
    qk = jnp.dot(xn, win_ref[:, 0:2 * A_INNER], preferred_element_type=F32)
    ss = jnp.dot((qk * qk).astype(BF16), seg_ref[...], preferred_element_type=F32)
    inv = lax.rsqrt(ss * (1.0 / A_HD) + RMS_EPS)
    inv_hi = inv.astype(BF16)
    inv_lo = (inv - inv_hi.astype(F32)).astype(BF16)
    inv_e = jnp.dot(jnp.concatenate([inv_hi, inv_lo], axis=1), exp_ref[...],
                    preferred_element_type=F32)
    qkn = qk * inv_e * gain_ref[...]
    q_ref[...] = qkn[:, 0:A_INNER].astype(BF16)
    k_ref[...] = qkn[:, A_INNER:2 * A_INNER].astype(BF16)
    v_ref[...] = jnp.dot(xn, win_ref[:, 2 * A_INNER:3 * A_INNER],
                         preferred_element_type=F32).astype(BF16)
    z = jnp.dot(xn, win_ref[:, 3 * A_INNER:4 * A_INNER], preferred_element_type=F32)
    sz_ref[...] = (z * _sigmoid(z)).astype(BF16)


def _attn_pre(x, g, w_in, gain, seg, expand):
    B, T, D = x.shape
    bm = min(APRE_BM, T)
    nt = T // bm
    full = lambda *shape: pl.BlockSpec(shape, lambda b, t: (0,) * len(shape))
    tok = lambda width: pl.BlockSpec((None, bm, width), lambda b, t: (b, t, 0))
    o = jax.ShapeDtypeStruct((B, T, A_INNER), BF16)
    return pl.pallas_call(
        _attn_pre_kernel,
        grid=(B, nt),
        in_specs=[tok(D), full(1, D), full(D, 4 * A_INNER), full(1, 2 * A_INNER),
                  full(2 * A_INNER, LANES), full(2 * LANES, 2 * A_INNER)],
        out_specs=(tok(A_INNER),) * 4,
        out_shape=(o, o, o, o),
        compiler_params=pltpu.CompilerParams(
            dimension_semantics=("parallel", "parallel"), vmem_limit_bytes=VMEM_LIMIT),
        name="attn_pre",
    )(x, g, w_in, gain, seg, expand)


def _attn_kernel(q_ref, kp_ref, kc_ref, vp_ref, vc_ref, sz_ref, x_ref, bias_ref, wout_ref,
                 out_ref,
                 kcat, vcat, o_s, s_even, s_odd):
    i = pl.program_id(1)
    bm = q_ref.shape[0]
    n_chunks = bm // CHUNK
    kcat[0:bm, :] = kp_ref[...]
    kcat[bm:2 * bm, :] = kc_ref[...]
    vcat[0:bm, :] = vp_ref[...]
    vcat[bm:2 * bm, :] = vc_ref[...]
    first_head = lax.broadcasted_iota(jnp.int32, (CHUNK, 2 * A_HD), 1) < A_HD
    key_off = lax.broadcasted_iota(jnp.int32, (1, BAND), 1)

    lanes = lambda j: slice(j * 2 * A_HD, (j + 1) * 2 * A_HD)

    def scores(c, j, s_ref):
        r0 = pl.multiple_of(c * CHUNK, CHUNK)
        q2 = q_ref[pl.ds(r0, CHUNK), lanes(j)].astype(F32)
        wt = jnp.concatenate([jnp.where(first_head, q2, 0.0),
                              jnp.where(first_head, 0.0, q2)], axis=0).astype(BF16)
        k2 = kcat[pl.ds(r0 + bm - LEFT_CHUNKS * CHUNK, BAND), lanes(j)]
        s_ref[j] = lax.dot_general(wt, k2, NT_DIMS, preferred_element_type=F32)

    def finish(masked, c, j, s_ref):
        r0 = pl.multiple_of(c * CHUNK, CHUNK)
        s = s_ref[j] + bias_ref[j]
        if masked:
            valid = (c * CHUNK - LEFT_CHUNKS * CHUNK + key_off) >= 0
            s = jnp.where(valid, s, -jnp.inf)
        e = jnp.exp(s - jnp.max(s, axis=1, keepdims=True))
        l = jnp.sum(e, axis=1, keepdims=True)
        v2 = vcat[pl.ds(r0 + bm - LEFT_CHUNKS * CHUNK, BAND), lanes(j)]
        r = jnp.dot(e.astype(BF16), v2, preferred_element_type=F32) * (1.0 / l)
        o2 = jnp.where(first_head, r[0:CHUNK, :], r[CHUNK:2 * CHUNK, :])
        o_s[pl.ds(r0, CHUNK), lanes(j)] = (
            o2 * sz_ref[pl.ds(r0, CHUNK), lanes(j)].astype(F32)).astype(BF16)

    def chunk(masked, c, s_cur, s_nxt):
        for j in range(A_PAIRS):
            finish(masked, c, j, s_cur)
            if s_nxt is not None:
                scores(c + 1, j, s_nxt)

    def run(masked):
        for j in range(A_PAIRS):
            scores(0, j, s_even)

        def two_chunks(cc, carry):
            chunk(masked, 2 * cc, s_even, s_odd)
            chunk(masked, 2 * cc + 1, s_odd, s_even)
            return carry

        lax.fori_loop(0, n_chunks // 2 - 1, two_chunks, 0)
        chunk(masked, n_chunks - 2, s_even, s_odd)
        chunk(masked, n_chunks - 1, s_odd, None)

    @pl.when(i == 0)
    def _():
        run(True)

    @pl.when(i > 0)
    def _():
        run(False)

    out_ref[...] = x_ref[...] + jnp.dot(o_s[...], wout_ref[...], preferred_element_type=F32)


def _attn(q, k, v, sz, x, bias, w_out):
    B, T, D = x.shape
    bm = ATT_BM
    nt = T // bm
    full = lambda *shape: pl.BlockSpec(shape, lambda b, t: (0,) * len(shape))
    cur = lambda width: pl.BlockSpec((None, bm, width), lambda b, t: (b, t, 0))
    prev = lambda width: pl.BlockSpec((None, bm, width), lambda b, t: (b, jnp.maximum(t - 1, 0), 0))
    return pl.pallas_call(
        _attn_kernel,
        grid=(B, nt),
        in_specs=[cur(A_INNER), prev(A_INNER), cur(A_INNER), prev(A_INNER), cur(A_INNER),
                  cur(A_INNER), cur(D), full(A_PAIRS, 2 * CHUNK, BAND), full(A_INNER, D)],
        out_specs=cur(D),
        out_shape=jax.ShapeDtypeStruct((B, T, D), F32),
        scratch_shapes=[
            pltpu.VMEM((2 * bm, A_INNER), BF16),
            pltpu.VMEM((2 * bm, A_INNER), BF16),
            pltpu.VMEM((bm, A_INNER), BF16),
            pltpu.VMEM((A_PAIRS, 2 * CHUNK, BAND), F32),
            pltpu.VMEM((A_PAIRS, 2 * CHUNK, BAND), F32),
        ],
        compiler_params=pltpu.CompilerParams(
            dimension_semantics=("parallel", "parallel"), vmem_limit_bytes=VMEM_LIMIT),
        name="attn",
    )(q, k, k, v, v, sz, x, bias, w_out)


def _qk_norm_constants():
    col = jnp.arange(2 * A_INNER) // A_HD
    seg = (col[:, None] == jnp.arange(LANES)[None, :]).astype(BF16)
    row = jnp.arange(2 * LANES) % LANES
    expand = (row[:, None] == col[None, :]).astype(BF16)
    return seg, expand


def kernel(x, norm_g, a_w_in, a_conv_w, a_conv_b, a_w_q, a_w_k, a_w_v, a_w_o, a_w_if, a_b_if,
           a_out_g, a_skip, a_w_out, b_w_in, b_q_g, b_k_g, b_rel_bias, b_w_out):
    assert x.shape[1] % ATT_BM == 0 and x.shape[2] == D_MODEL
    bf = lambda w: w.astype(BF16)
    pad_gate = lambda w: jnp.pad(w, ((0, 0), (0, GATE_PAD - 2 * M_HEADS)))

    q, k, v, o, c, sz, gc, gr = _mlstm_pre(
        x, norm_g[0][None, :], bf(a_w_in[0]), a_conv_w[0], a_conv_b[0][None, :],
        bf(a_w_q[0]), bf(a_w_k[0]), bf(a_w_v[0]), bf(a_w_o[0]),
        bf(pad_gate(a_w_if[0])), pad_gate(a_b_if[0][None, :]))
    x1 = _mlstm_rec(q, k, v, o, c, sz, x, gc, gr,
                    a_out_g[0].reshape(1, M_INNER), a_skip[0][None, :], bf(a_w_out[0]))

    bias = _attn_bias(b_rel_bias[0])
    gain = jnp.concatenate([jnp.tile(b_q_g[0], A_HEADS) * (A_HD ** -0.5),
                            jnp.tile(b_k_g[0], A_HEADS)])[None, :]
    seg, expand = _qk_norm_constants()
    q, k, v, sz = _attn_pre(x1, norm_g[1][None, :], bf(b_w_in[0]), gain, seg, expand)
    return _attn(q, k, v, sz, x1, bias, bf(b_w_out[0]))
```

```python
import functools

import jax
import jax.numpy as jnp
from jax import lax
from jax.experimental import pallas as pl
from jax.experimental.pallas import tpu as pltpu

F32 = jnp.float32
BF16 = jnp.bfloat16

RMS_EPS = 1e-6
CHUNK = 64
LOG2_E = 1.4426950408889634

D_MODEL = 1024
M_INNER = 2 * D_MODEL
M_HEADS = 4
M_DV = M_INNER // M_HEADS
M_DQK = M_DV // 2
CONV_K = 4

A_INNER = D_MODEL
A_HEADS = 16
A_HD = A_INNER // A_HEADS
A_PAIRS = A_HEADS // 2
LEFT_CHUNKS = 8
BAND = (LEFT_CHUNKS + 1) * CHUNK
REL_PAST = 256
REL_FUTURE = CHUNK - 1
REL_SIZE = REL_PAST + REL_FUTURE + 1

LANES = 128
SUBLANES = 8
GATE_PAD = LANES
VMEM_LIMIT = 56 * 1024 * 1024

PRE_BM = 256
REC_L = 256
APRE_BM = 256
ATT_BM = LEFT_CHUNKS * CHUNK
BIAS_W = 640

NT_DIMS = (((1,), (1,)), ((), ()))
TN_DIMS = (((0,), (0,)), ((), ()))


def _sigmoid(x):
    return 1.0 / (1.0 + jnp.exp(-x))


def _log_sigmoid(x):
    return jnp.minimum(x, 0.0) - jnp.log1p(jnp.exp(-jnp.abs(x)))


def _rms(x, eps=RMS_EPS):
    return x * lax.rsqrt(jnp.mean(x * x, axis=-1, keepdims=True) + eps)


def _mlstm_pre_kernel(x_ref, g_ref, win_ref, cw_ref, cb_ref, wq_ref, wk_ref, wv_ref, wo_ref,
                      wif_ref, bif_ref, skip_ref,
                      q_ref, k_ref, v_ref, o_ref, cs_ref, sz_ref, gc_ref, gr_ref,
                      ubuf):
    t = pl.program_id(1)
    bm = x_ref.shape[0]

    @pl.when(t == 0)
    def _():
        ubuf[0:SUBLANES, :] = jnp.zeros((SUBLANES, M_INNER), F32)

    xn = (_rms(x_ref[...]) * g_ref[...]).astype(BF16)
    gates = jnp.zeros((bm, GATE_PAD), F32) + bif_ref[...]
    for h in range(M_HEADS):
        lo, hi = h * M_DV, (h + 1) * M_DV
        u = jnp.dot(xn, win_ref[:, lo:hi], preferred_element_type=F32)
        z = jnp.dot(xn, win_ref[:, M_INNER + lo:M_INNER + hi], preferred_element_type=F32)
        ubuf[SUBLANES:SUBLANES + bm, lo:hi] = u
        conv = cb_ref[:, lo:hi] + cw_ref[CONV_K - 1:CONV_K, lo:hi] * u
        for j in range(CONV_K - 1):
            shifted = ubuf[pl.ds(SUBLANES - (CONV_K - 1) + j, bm), lo:hi]
            conv = conv + cw_ref[j:j + 1, lo:hi] * shifted
        ubuf[0:SUBLANES, lo:hi] = u[bm - SUBLANES:bm, :]
        c = conv * _sigmoid(conv)
        cb16 = c.astype(BF16)
        ub16 = u.astype(BF16)
        q_ref[:, h * M_DQK:(h + 1) * M_DQK] = jnp.dot(
            cb16, wq_ref[h], preferred_element_type=F32).astype(BF16)
        k_ref[:, h * M_DQK:(h + 1) * M_DQK] = (jnp.dot(
            cb16, wk_ref[h], preferred_element_type=F32) * (M_DQK ** -0.5)).astype(BF16)
        v_ref[:, lo:hi] = jnp.dot(ub16, wv_ref[h], preferred_element_type=F32).astype(BF16)
        o_ref[:, lo:hi] = _sigmoid(jnp.dot(ub16, wo_ref[h], preferred_element_type=F32)).astype(BF16)
        cs_ref[:, lo:hi] = (skip_ref[:, lo:hi] * c).astype(BF16)
        sz_ref[:, lo:hi] = (z * _sigmoid(z)).astype(BF16)
        gates = gates + jnp.dot(cb16, wif_ref[lo:hi, :], preferred_element_type=F32)

    lf = _log_sigmoid(gates)
    lf_hi = lf.astype(BF16)
    rem = lf - lf_hi.astype(F32)
    lf_mid = rem.astype(BF16)
    lf_lo = (rem - lf_mid.astype(F32)).astype(BF16)
    tril = (lax.broadcasted_iota(jnp.int32, (bm, bm), 1)
            <= lax.broadcasted_iota(jnp.int32, (bm, bm), 0)).astype(BF16)
    cum = (jnp.dot(tril, lf_hi, preferred_element_type=F32)
           + jnp.dot(tril, lf_mid, preferred_element_type=F32)
           + jnp.dot(tril, lf_lo, preferred_element_type=F32))
    b = pltpu.roll(cum, GATE_PAD - M_HEADS, 1)
    ar = gates - b
    row = lax.broadcasted_iota(jnp.int32, (bm, GATE_PAD), 0)
    cm = ar
    shift = 1
    while shift < bm:
        cm = jnp.maximum(cm, jnp.where(row >= shift, pltpu.roll(cm, shift, 0), -jnp.inf))
        shift *= 2
    lane = lax.broadcasted_iota(jnp.int32, (bm, GATE_PAD), 1)
    gc_ref[...] = jnp.where(lane < M_HEADS, b,
                            jnp.where(lane < 2 * M_HEADS, pltpu.roll(ar, M_HEADS, 1),
                                      pltpu.roll(cm, 2 * M_HEADS, 1)))
    gr_ref[...] = ar.T[0:2 * M_HEADS, :]


def _mlstm_pre(x, g, w_in, conv_w, conv_b, w_q, w_k, w_v, w_o, w_if, b_if, skip):
    B, T, D = x.shape
    bm = min(PRE_BM, T)
    nt = T // bm
    full = lambda *shape: pl.BlockSpec(shape, lambda b, t: (0,) * len(shape))
    tok = lambda width: pl.BlockSpec((None, bm, width), lambda b, t: (b, t, 0))
    out_shape = (
        jax.ShapeDtypeStruct((B, T, M_HEADS * M_DQK), BF16),
        jax.ShapeDtypeStruct((B, T, M_HEADS * M_DQK), BF16),
        jax.ShapeDtypeStruct((B, T, M_INNER), BF16),
        jax.ShapeDtypeStruct((B, T, M_INNER), BF16),
        jax.ShapeDtypeStruct((B, T, M_INNER), BF16),
        jax.ShapeDtypeStruct((B, T, M_INNER), BF16),
        jax.ShapeDtypeStruct((B, T, GATE_PAD), F32),
        jax.ShapeDtypeStruct((B, 2 * M_HEADS, T), F32),
    )
    return pl.pallas_call(
        _mlstm_pre_kernel,
        grid=(B, nt),
        in_specs=[
            tok(D), full(1, D), full(D, 2 * M_INNER), full(CONV_K, M_INNER), full(1, M_INNER),
            full(M_HEADS, M_DV, M_DQK), full(M_HEADS, M_DV, M_DQK),
            full(M_HEADS, M_DV, M_DV), full(M_HEADS, M_DV, M_DV),
            full(M_INNER, GATE_PAD), full(1, GATE_PAD), full(1, M_INNER),
        ],
        out_specs=(
            tok(M_HEADS * M_DQK), tok(M_HEADS * M_DQK), tok(M_INNER), tok(M_INNER), tok(M_INNER),
            tok(M_INNER), tok(GATE_PAD),
            pl.BlockSpec((None, 2 * M_HEADS, bm), lambda b, t: (b, 0, t)),
        ),
        out_shape=out_shape,
        scratch_shapes=[pltpu.VMEM((SUBLANES + bm, M_INNER), F32)],
        compiler_params=pltpu.CompilerParams(
            dimension_semantics=("parallel", "arbitrary"), vmem_limit_bytes=VMEM_LIMIT),
        name="mlstm_pre",
    )(x, g, w_in, conv_w, conv_b, w_q, w_k, w_v, w_o, w_if, b_if, skip)


def _mlstm_rec_kernel(q_ref, k_ref, v_ref, o_ref, cs_ref, sz_ref, x_ref, gc_ref, gr_ref,
                      og_ref, wout_ref,
                      out_ref,
                      ctx_ref, ctxb_ref, m_ref, s_scr, qc_scr, p_scr, kw_scr, pv_scr, cu_scr, y_scr):
    t = pl.program_id(1)
    L = q_ref.shape[0]

    @pl.when(t == 0)
    def _():
        ctx_ref[...] = jnp.zeros(ctx_ref.shape, F32)
        ctxb_ref[...] = jnp.zeros(ctxb_ref.shape, BF16)
        m_ref[...] = jnp.zeros(m_ref.shape, F32)

    heads = range(M_HEADS)
    qk = lambda h: slice(h * M_DQK, (h + 1) * M_DQK)
    dv = lambda h: slice(h * M_DV, (h + 1) * M_DV)
    ones_tile = jnp.ones((L, LANES), BF16)
    vx = lambda h: jnp.concatenate([v_ref[:, dv(h)], ones_tile], axis=1)

    for h in heads:
        qh = q_ref[:, qk(h)]
        s_scr[h] = lax.dot_general(qh, k_ref[:, qk(h)], NT_DIMS, preferred_element_type=F32)
        qc_scr[h] = jnp.dot(qh, ctxb_ref[h], preferred_element_type=F32)

    gc = gc_ref[...]
    b = gc
    ar = pltpu.roll(gc, GATE_PAD - M_HEADS, 1)
    cm = pltpu.roll(gc, GATE_PAD - 2 * M_HEADS, 1)
    m_prev = m_ref[...]
    g_last = b[L - 1:L, :]
    a_c = g_last + ar
    m_new = jnp.maximum(g_last + m_prev, jnp.max(a_c, axis=0, keepdims=True))
    w_all = jnp.exp(a_c - m_new)
    decay = jnp.exp(g_last + m_prev - m_new)
    mx = jnp.maximum(m_prev, cm)
    inter_all = jnp.exp(m_prev - mx)
    enmt_all = jnp.exp(-(b + mx))
    m_ref[...] = m_new
    gr = gr_ref[...]
    tril = (lax.broadcasted_iota(jnp.int32, (L, L), 1)
            <= lax.broadcasted_iota(jnp.int32, (L, L), 0))

    for h in heads:
        d = jnp.where(tril, jnp.exp(gr[h:h + 1, :] - mx[:, h:h + 1]), 0.0)
        p_scr[h] = (s_scr[h] * d).astype(BF16)
        kw_scr[h] = (k_ref[:, qk(h)].astype(F32) * w_all[:, h:h + 1]).astype(BF16)

    for h in heads:
        pv_scr[h] = jnp.dot(p_scr[h], vx(h), preferred_element_type=F32)
        cu_scr[h] = lax.dot_general(kw_scr[h], vx(h), TN_DIMS, preferred_element_type=F32)

    for h in heads:
        inter = jnp.broadcast_to(inter_all[:, h:h + 1], (L, LANES))
        enmt = jnp.broadcast_to(enmt_all[:, h:h + 1], (L, LANES))
        num = pv_scr[h, :, 0:M_DV] + jnp.tile(inter, (1, M_DV // LANES)) * qc_scr[h, :, 0:M_DV]
        den = pv_scr[h, :, M_DV:M_DV + LANES] + inter * qc_scr[h, :, M_DV:M_DV + LANES]
        r = 1.0 / jnp.maximum(jnp.abs(den), enmt)
        ms = jnp.mean(num * num, axis=1, keepdims=True)
        scale = r * lax.rsqrt(r * r * ms + RMS_EPS)
        hn = (num * jnp.tile(scale, (1, M_DV // LANES)) * og_ref[:, dv(h)]).astype(BF16)
        y_scr[:, dv(h)] = (o_ref[:, dv(h)] * hn + cs_ref[:, dv(h)]) * sz_ref[:, dv(h)]

    out_ref[...] = x_ref[...] + jnp.dot(y_scr[...], wout_ref[...], preferred_element_type=F32)

    for h in heads:
        ctx = decay[:, h:h + 1] * ctx_ref[h] + cu_scr[h]
        ctx_ref[h] = ctx
        ctxb_ref[h] = ctx.astype(BF16)


def _mlstm_rec(q, k, v, o, cs, sz, x, gc, gr, out_g, w_out):
    B, T, D = x.shape
    L = min(REC_L, T)
    nt = T // L
    dvx = M_DV + LANES
    full = lambda *shape: pl.BlockSpec(shape, lambda b, t: (0,) * len(shape))
    tok = lambda width: pl.BlockSpec((None, L, width), lambda b, t: (b, t, 0))
    return pl.pallas_call(
        _mlstm_rec_kernel,
        grid=(B, nt),
        in_specs=[
            tok(M_HEADS * M_DQK), tok(M_HEADS * M_DQK), tok(M_INNER), tok(M_INNER), tok(M_INNER),
            tok(M_INNER), tok(D), tok(GATE_PAD),
            pl.BlockSpec((None, 2 * M_HEADS, L), lambda b, t: (b, 0, t)),
            full(1, M_INNER), full(M_INNER, D),
        ],
        out_specs=tok(D),
        out_shape=jax.ShapeDtypeStruct((B, T, D), F32),
        scratch_shapes=[
            pltpu.VMEM((M_HEADS, M_DQK, dvx), F32),
            pltpu.VMEM((M_HEADS, M_DQK, dvx), BF16),
            pltpu.VMEM((1, LANES), F32),
            pltpu.VMEM((M_HEADS, L, L), F32),
            pltpu.VMEM((M_HEADS, L, dvx), F32),
            pltpu.VMEM((M_HEADS, L, L), BF16),
            pltpu.VMEM((M_HEADS, L, M_DQK), BF16),
            pltpu.VMEM((M_HEADS, L, dvx), F32),
            pltpu.VMEM((M_HEADS, M_DQK, dvx), F32),
            pltpu.VMEM((L, M_INNER), BF16),
        ],
        compiler_params=pltpu.CompilerParams(
            dimension_semantics=("parallel", "arbitrary"), vmem_limit_bytes=VMEM_LIMIT),
        name="mlstm_rec",
    )(q, k, v, o, cs, sz, x, gc, gr, out_g, w_out)


def _bias_kernel(tab_ref, out_ref):
    r = lax.broadcasted_iota(jnp.int32, (REL_SIZE, BIAS_W), 0)
    e = lax.broadcasted_iota(jnp.int32, (REL_SIZE, BIAS_W), 1)
    e = jnp.where(e >= BAND, e - BIAS_W, e)
    idx = jnp.clip(LEFT_CHUNKS * CHUNK - e, -REL_FUTURE, REL_PAST) + REL_FUTURE
    onehot = (r == idx).astype(F32)
    g = jnp.dot(tab_ref[...], onehot, precision=lax.Precision.HIGHEST,
                preferred_element_type=F32) * LOG2_E
    for j in range(A_PAIRS):
        halves = []
        for a in range(2):
            rows = jnp.broadcast_to(g[2 * j + a:2 * j + a + 1, :], (CHUNK, BIAS_W))
            halves.append(pltpu.roll(rows, 0, 1, stride=1, stride_axis=0))
        both = jnp.concatenate(halves, axis=0)
        out_ref[j] = both[:, 0:BAND]


def _attn_bias(rel_bias):
    return pl.pallas_call(
        _bias_kernel,
        out_shape=jax.ShapeDtypeStruct((A_PAIRS, 2 * CHUNK, BAND), F32),
        compiler_params=pltpu.CompilerParams(vmem_limit_bytes=VMEM_LIMIT),
        name="attn_bias",
    )(rel_bias)


def _attn_pre_kernel(x_ref, g_ref, win_ref, gain_ref, seg_ref, exp_ref,
                     q_ref, k_ref, v_ref, sz_ref):
    xn = (_rms(x_ref[...]) * g_ref[...]).astype(BF16)
    qk = jnp.dot(xn, win_ref[:, 0:2 * A_INNER], preferred_element_type=F32)
    ss = jnp.dot((qk * qk).astype(BF16), seg_ref[...], preferred_element_type=F32)
    inv = lax.rsqrt(ss * (1.0 / A_HD) + RMS_EPS)
    inv_hi = inv.astype(BF16)
    inv_lo = (inv - inv_hi.astype(F32)).astype(BF16)
    inv_e = jnp.dot(jnp.concatenate([inv_hi, inv_lo], axis=1), exp_ref[...],
                    preferred_element_type=F32)
    qkn = qk * inv_e * gain_ref[...]
    q_ref[...] = qkn[:, 0:A_INNER].astype(BF16)
    k_ref[...] = qkn[:, A_INNER:2 * A_INNER].astype(BF16)
    v_ref[...] = jnp.dot(xn, win_ref[:, 2 * A_INNER:3 * A_INNER],
                         preferred_element_type=F32).astype(BF16)
    z = jnp.dot(xn, win_ref[:, 3 * A_INNER:4 * A_INNER], preferred_element_type=F32)
    sz_ref[...] = (z * _sigmoid(z)).astype(BF16)


def _attn_pre(x, g, w_in, gain, seg, expand):
    B, T, D = x.shape
    bm = min(APRE_BM, T)
    nt = T // bm
    full = lambda *shape: pl.BlockSpec(shape, lambda b, t: (0,) * len(shape))
    tok = lambda width: pl.BlockSpec((None, bm, width), lambda b, t: (b, t, 0))
    o = jax.ShapeDtypeStruct((B, T, A_INNER), BF16)
    return pl.pallas_call(
        _attn_pre_kernel,
        grid=(B, nt),
        in_specs=[tok(D), full(1, D), full(D, 4 * A_INNER), full(1, 2 * A_INNER),
                  full(2 * A_INNER, LANES), full(2 * LANES, 2 * A_INNER)],
        out_specs=(tok(A_INNER),) * 4,
        out_shape=(o, o, o, o),
        compiler_params=pltpu.CompilerParams(
            dimension_semantics=("parallel", "parallel"), vmem_limit_bytes=VMEM_LIMIT),
        name="attn_pre",
    )(x, g, w_in, gain, seg, expand)


def _attn_kernel(q_ref, kp_ref, kc_ref, vp_ref, vc_ref, sz_ref, x_ref, bias_ref, wout_ref,
                 out_ref,
                 kcat, vcat, o_s, s_even, s_odd):
    i = pl.program_id(1)
    bm = q_ref.shape[0]
    n_chunks = bm // CHUNK
    kcat[0:bm, :] = kp_ref[...]
    kcat[bm:2 * bm, :] = kc_ref[...]
    vcat[0:bm, :] = vp_ref[...]
    vcat[bm:2 * bm, :] = vc_ref[...]
    first_head = lax.broadcasted_iota(jnp.int32, (CHUNK, 2 * A_HD), 1) < A_HD
    key_off = lax.broadcasted_iota(jnp.int32, (1, BAND), 1)

    lanes = lambda j: slice(j * 2 * A_HD, (j + 1) * 2 * A_HD)

    def scores(masked, c, j, s_ref):
        r0 = pl.multiple_of(c * CHUNK, CHUNK)
        q2 = q_ref[pl.ds(r0, CHUNK), lanes(j)].astype(F32)
        wt = jnp.concatenate([jnp.where(first_head, q2, 0.0),
                              jnp.where(first_head, 0.0, q2)], axis=0).astype(BF16)
        k2 = kcat[pl.ds(r0 + bm - LEFT_CHUNKS * CHUNK, BAND), lanes(j)]
        s = lax.dot_general(wt, k2, NT_DIMS, preferred_element_type=F32) + bias_ref[j]
        if masked:
            valid = (c * CHUNK - LEFT_CHUNKS * CHUNK + key_off) >= 0
            s = jnp.where(valid, s, -jnp.inf)
        s_ref[j] = s

    def finish(c, j, s_ref):
        r0 = pl.multiple_of(c * CHUNK, CHUNK)
        e = jnp.exp2(s_ref[j] - jnp.max(s_ref[j], axis=1, keepdims=True))
        l = jnp.sum(e, axis=1, keepdims=True)
        v2 = vcat[pl.ds(r0 + bm - LEFT_CHUNKS * CHUNK, BAND), lanes(j)]
        r = jnp.dot(e.astype(BF16), v2, preferred_element_type=F32) * (1.0 / l)
        o2 = jnp.where(first_head, r[0:CHUNK, :], r[CHUNK:2 * CHUNK, :])
        o_s[pl.ds(r0, CHUNK), lanes(j)] = (
            o2 * sz_ref[pl.ds(r0, CHUNK), lanes(j)].astype(F32)).astype(BF16)

    def chunk(masked, c, s_cur, s_nxt):
        for j in range(A_PAIRS):
            finish(c, j, s_cur)
            if s_nxt is not None:
                scores(masked, c + 1, j, s_nxt)

    def run(masked):
        for j in range(A_PAIRS):
            scores(masked, 0, j, s_even)

        def two_chunks(cc, carry):
            chunk(masked, 2 * cc, s_even, s_odd)
            chunk(masked, 2 * cc + 1, s_odd, s_even)
            return carry

        lax.fori_loop(0, n_chunks // 2 - 1, two_chunks, 0)
        chunk(masked, n_chunks - 2, s_even, s_odd)
        chunk(masked, n_chunks - 1, s_odd, None)

    @pl.when(i == 0)
    def _():
        run(True)

    @pl.when(i > 0)
    def _():
        run(False)

    out_ref[...] = x_ref[...] + jnp.dot(o_s[...], wout_ref[...], preferred_element_type=F32)


def _attn(q, k, v, sz, x, bias, w_out):
    B, T, D = x.shape
    bm = ATT_BM
    nt = T // bm
    full = lambda *shape: pl.BlockSpec(shape, lambda b, t: (0,) * len(shape))
    cur = lambda width: pl.BlockSpec((None, bm, width), lambda b, t: (b, t, 0))
    prev = lambda width: pl.BlockSpec((None, bm, width), lambda b, t: (b, jnp.maximum(t - 1, 0), 0))
    return pl.pallas_call(
        _attn_kernel,
        grid=(B, nt),
        in_specs=[cur(A_INNER), prev(A_INNER), cur(A_INNER), prev(A_INNER), cur(A_INNER),
                  cur(A_INNER), cur(D), full(A_PAIRS, 2 * CHUNK, BAND), full(A_INNER, D)],
        out_specs=cur(D),
        out_shape=jax.ShapeDtypeStruct((B, T, D), F32),
        scratch_shapes=[
            pltpu.VMEM((2 * bm, A_INNER), BF16),
            pltpu.VMEM((2 * bm, A_INNER), BF16),
            pltpu.VMEM((bm, A_INNER), BF16),
            pltpu.VMEM((A_PAIRS, 2 * CHUNK, BAND), F32),
            pltpu.VMEM((A_PAIRS, 2 * CHUNK, BAND), F32),
        ],
        compiler_params=pltpu.CompilerParams(
            dimension_semantics=("parallel", "parallel"), vmem_limit_bytes=VMEM_LIMIT),
        name="attn",
    )(q, k, k, v, v, sz, x, bias, w_out)


def _qk_norm_constants():
    col = jnp.arange(2 * A_INNER) // A_HD
    seg = (col[:, None] == jnp.arange(LANES)[None, :]).astype(BF16)
    row = jnp.arange(2 * LANES) % LANES
    expand = (row[:, None] == col[None, :]).astype(BF16)
    return seg, expand


def kernel(x, norm_g, a_w_in, a_conv_w, a_conv_b, a_w_q, a_w_k, a_w_v, a_w_o, a_w_if, a_b_if,
           a_out_g, a_skip, a_w_out, b_w_in, b_q_g, b_k_g, b_rel_bias, b_w_out):
    assert x.shape[1] % ATT_BM == 0 and x.shape[2] == D_MODEL
    bf = lambda w: w.astype(BF16)
    pad_gate = lambda w: jnp.pad(w, ((0, 0), (0, GATE_PAD - 2 * M_HEADS)))

    q, k, v, o, cs, sz, gc, gr = _mlstm_pre(
        x, norm_g[0][None, :], bf(a_w_in[0]), a_conv_w[0], a_conv_b[0][None, :],
        bf(a_w_q[0]), bf(a_w_k[0]), bf(a_w_v[0]), bf(a_w_o[0]),
        bf(pad_gate(a_w_if[0])), pad_gate(a_b_if[0][None, :]), a_skip[0][None, :])
    x1 = _mlstm_rec(q, k, v, o, cs, sz, x, gc, gr, a_out_g[0].reshape(1, M_INNER), bf(a_w_out[0]))

    bias = _attn_bias(b_rel_bias[0])
    gain = jnp.concatenate([jnp.tile(b_q_g[0], A_HEADS) * (A_HD ** -0.5 * LOG2_E),
                            jnp.tile(b_k_g[0], A_HEADS)])[None, :]
    seg, expand = _qk_norm_constants()
    q, k, v, sz = _attn_pre(x1, norm_g[1][None, :], bf(b_w_in[0]), gain, seg, expand)
    return _attn(q, k, v, sz, x1, bias, bf(b_w_out[0]))
```

```python
import functools

import jax
import jax.numpy as jnp
from jax import lax
from jax.experimental import pallas as pl
from jax.experimental.pallas import tpu as pltpu

F32 = jnp.float32
BF16 = jnp.bfloat16

RMS_EPS = 1e-6
CHUNK = 64
LOG2_E = 1.4426950408889634

D_MODEL = 1024
M_INNER = 2 * D_MODEL
M_HEADS = 4
M_DV = M_INNER // M_HEADS
M_DQK = M_DV // 2
CONV_K = 4

A_INNER = D_MODEL
A_HEADS = 16
A_HD = A_INNER // A_HEADS
A_PAIRS = A_HEADS // 2
LEFT_CHUNKS = 8
BAND = (LEFT_CHUNKS + 1) * CHUNK
REL_PAST = 256
REL_FUTURE = CHUNK - 1
REL_SIZE = REL_PAST + REL_FUTURE + 1

LANES = 128
SUBLANES = 8
GATE_PAD = LANES
VMEM_LIMIT = 56 * 1024 * 1024
PRE_BM = 256
REC_L = 256
APRE_BM = 256
ATT_BM = LEFT_CHUNKS * CHUNK
BIAS_W = 640

NT_DIMS = (((1,), (1,)), ((), ()))
TN_DIMS = (((0,), (0,)), ((), ()))


def _sigmoid(x):
    return 1.0 / (1.0 + jnp.exp(-x))


def _log_sigmoid(x):
    return jnp.minimum(x, 0.0) - jnp.log1p(jnp.exp(-jnp.abs(x)))


def _rms(x, eps=RMS_EPS):
    return x * lax.rsqrt(jnp.mean(x * x, axis=-1, keepdims=True) + eps)


def _mlstm_pre_kernel(x_ref, g_ref, win_ref, cw_ref, cb_ref, wq_ref, wk_ref, wv_ref, wo_ref,
                      wif_ref, bif_ref, skip_ref,
                      q_ref, k_ref, v_ref, o_ref, cs_ref, sz_ref, gc_ref, gr_ref,
                      ubuf, ub_scr, cb_scr):
    t = pl.program_id(1)
    bm = x_ref.shape[0]
    dv = lambda h: slice(h * M_DV, (h + 1) * M_DV)
    qk = lambda h: slice(h * M_DQK, (h + 1) * M_DQK)

    @pl.when(t == 0)
    def _():
        ubuf[0:SUBLANES, :] = jnp.zeros((SUBLANES, M_INNER), F32)

    xn = (_rms(x_ref[...]) * g_ref[...]).astype(BF16)

    for h in range(M_HEADS):
        ubuf[SUBLANES:SUBLANES + bm, dv(h)] = jnp.dot(xn, win_ref[:, dv(h)],
                                                      preferred_element_type=F32)

    def gate_branch(h):
        z = jnp.dot(xn, win_ref[:, M_INNER + h * M_DV:M_INNER + (h + 1) * M_DV],
                    preferred_element_type=F32)
        sz_ref[:, dv(h)] = (z * _sigmoid(z)).astype(BF16)

    def conv_branch(h):
        u = ubuf[SUBLANES:SUBLANES + bm, dv(h)]
        conv = cb_ref[:, dv(h)] + cw_ref[CONV_K - 1:CONV_K, dv(h)] * u
        for j in range(CONV_K - 1):
            shifted = ubuf[pl.ds(SUBLANES - (CONV_K - 1) + j, bm), dv(h)]
            conv = conv + cw_ref[j:j + 1, dv(h)] * shifted
        c = conv * _sigmoid(conv)
        cb_scr[:, dv(h)] = c.astype(BF16)
        ub_scr[:, dv(h)] = u.astype(BF16)
        cs_ref[:, dv(h)] = (skip_ref[:, dv(h)] * c).astype(BF16)

    def projections(h):
        cb16, ub16 = cb_scr[:, dv(h)], ub_scr[:, dv(h)]
        q_ref[:, qk(h)] = jnp.dot(cb16, wq_ref[h], preferred_element_type=F32).astype(BF16)
        k_ref[:, qk(h)] = (jnp.dot(cb16, wk_ref[h], preferred_element_type=F32)
                           * (M_DQK ** -0.5)).astype(BF16)
        v_ref[:, dv(h)] = jnp.dot(ub16, wv_ref[h], preferred_element_type=F32).astype(BF16)
        o_ref[:, dv(h)] = _sigmoid(jnp.dot(ub16, wo_ref[h], preferred_element_type=F32)).astype(BF16)

    for h in range(M_HEADS):
        gate_branch(h)
        if h > 0:
            projections(h - 1)
        conv_branch(h)
    ubuf[0:SUBLANES, :] = ubuf[bm:bm + SUBLANES, :]

    gates = bif_ref[...] + jnp.dot(cb_scr[...], wif_ref[...], preferred_element_type=F32)

    lf = _log_sigmoid(gates)
    lf_hi = lf.astype(BF16)
    rem = lf - lf_hi.astype(F32)
    lf_mid = rem.astype(BF16)
    lf_lo = (rem - lf_mid.astype(F32)).astype(BF16)
    tril = (lax.broadcasted_iota(jnp.int32, (bm, bm), 1)
            <= lax.broadcasted_iota(jnp.int32, (bm, bm), 0)).astype(BF16)
    cum = (jnp.dot(tril, lf_hi, preferred_element_type=F32)
           + jnp.dot(tril, lf_mid, preferred_element_type=F32)
           + jnp.dot(tril, lf_lo, preferred_element_type=F32))
    b = pltpu.roll(cum, GATE_PAD - M_HEADS, 1)
    ar = gates - b
    row = lax.broadcasted_iota(jnp.int32, (bm, GATE_PAD), 0)
    cm = ar
    shift = 1
    while shift < bm:
        cm = jnp.maximum(cm, jnp.where(row >= shift, pltpu.roll(cm, shift, 0), -jnp.inf))
        shift *= 2
    lane = lax.broadcasted_iota(jnp.int32, (bm, GATE_PAD), 1)
    gc_ref[...] = jnp.where(lane < M_HEADS, b,
                            jnp.where(lane < 2 * M_HEADS, pltpu.roll(ar, M_HEADS, 1),
                                      pltpu.roll(cm, 2 * M_HEADS, 1)))
    gr_ref[...] = ar.T[0:2 * M_HEADS, :]

    projections(M_HEADS - 1)


def _mlstm_pre(x, g, w_in, conv_w, conv_b, w_q, w_k, w_v, w_o, w_if, b_if, skip):
    B, T, D = x.shape
    bm = min(PRE_BM, T)
    nt = T // bm
    full = lambda *shape: pl.BlockSpec(shape, lambda b, t: (0,) * len(shape))
    tok = lambda width: pl.BlockSpec((None, bm, width), lambda b, t: (b, t, 0))
    out_shape = (
        jax.ShapeDtypeStruct((B, T, M_HEADS * M_DQK), BF16),
        jax.ShapeDtypeStruct((B, T, M_HEADS * M_DQK), BF16),
        jax.ShapeDtypeStruct((B, T, M_INNER), BF16),
        jax.ShapeDtypeStruct((B, T, M_INNER), BF16),
        jax.ShapeDtypeStruct((B, T, M_INNER), BF16),
        jax.ShapeDtypeStruct((B, T, M_INNER), BF16),
        jax.ShapeDtypeStruct((B, T, GATE_PAD), F32),
        jax.ShapeDtypeStruct((B, 2 * M_HEADS, T), F32),
    )
    return pl.pallas_call(
        _mlstm_pre_kernel,
        grid=(B, nt),
        in_specs=[
            tok(D), full(1, D), full(D, 2 * M_INNER), full(CONV_K, M_INNER), full(1, M_INNER),
            full(M_HEADS, M_DV, M_DQK), full(M_HEADS, M_DV, M_DQK),
            full(M_HEADS, M_DV, M_DV), full(M_HEADS, M_DV, M_DV),
            full(M_INNER, GATE_PAD), full(1, GATE_PAD), full(1, M_INNER),
        ],
        out_specs=(
            tok(M_HEADS * M_DQK), tok(M_HEADS * M_DQK), tok(M_INNER), tok(M_INNER), tok(M_INNER),
            tok(M_INNER), tok(GATE_PAD),
            pl.BlockSpec((None, 2 * M_HEADS, bm), lambda b, t: (b, 0, t)),
        ),
        out_shape=out_shape,
        scratch_shapes=[pltpu.VMEM((SUBLANES + bm, M_INNER), F32),
                        pltpu.VMEM((bm, M_INNER), BF16),
                        pltpu.VMEM((bm, M_INNER), BF16)],
        compiler_params=pltpu.CompilerParams(
            dimension_semantics=("parallel", "arbitrary"), vmem_limit_bytes=VMEM_LIMIT),
        name="mlstm_pre",
    )(x, g, w_in, conv_w, conv_b, w_q, w_k, w_v, w_o, w_if, b_if, skip)


def _mlstm_rec_kernel(q_ref, k_ref, v_ref, o_ref, cs_ref, sz_ref, x_ref, gc_ref, gr_ref,
                      og_ref, wout_ref,
                      out_ref,
                      ctx_ref, ctxb_ref, m_ref, s_scr, qc_scr, p_scr, kw_scr, pv_scr, cu_scr, y_scr):
    t = pl.program_id(1)
    L = q_ref.shape[0]

    @pl.when(t == 0)
    def _():
        ctx_ref[...] = jnp.zeros(ctx_ref.shape, F32)
        ctxb_ref[...] = jnp.zeros(ctxb_ref.shape, BF16)
        m_ref[...] = jnp.zeros(m_ref.shape, F32)

    heads = range(M_HEADS)
    qk = lambda h: slice(h * M_DQK, (h + 1) * M_DQK)
    dv = lambda h: slice(h * M_DV, (h + 1) * M_DV)
    ones_tile = jnp.ones((L, LANES), BF16)
    vx = lambda h: jnp.concatenate([v_ref[:, dv(h)], ones_tile], axis=1)

    for h in heads:
        qh = q_ref[:, qk(h)]
        s_scr[h] = lax.dot_general(qh, k_ref[:, qk(h)], NT_DIMS, preferred_element_type=F32)
        qc_scr[h] = jnp.dot(qh, ctxb_ref[h], preferred_element_type=F32)

    gc = gc_ref[...]
    b = gc
    ar = pltpu.roll(gc, GATE_PAD - M_HEADS, 1)
    cm = pltpu.roll(gc, GATE_PAD - 2 * M_HEADS, 1)
    m_prev = m_ref[...]
    g_last = b[L - 1:L, :]
    a_c = g_last + ar
    m_new = jnp.maximum(g_last + m_prev, jnp.max(a_c, axis=0, keepdims=True))
    w_all = jnp.exp(a_c - m_new)
    decay = jnp.exp(g_last + m_prev - m_new)
    mx = jnp.maximum(m_prev, cm)
    inter_all = jnp.exp(m_prev - mx)
    enmt_all = jnp.exp(-(b + mx))
    m_ref[...] = m_new
    gr = gr_ref[...]
    tril = (lax.broadcasted_iota(jnp.int32, (L, L), 1)
            <= lax.broadcasted_iota(jnp.int32, (L, L), 0))

    for h in heads:
        d = jnp.where(tril, jnp.exp(gr[h:h + 1, :] - mx[:, h:h + 1]), 0.0)
        p_scr[h] = (s_scr[h] * d).astype(BF16)
        kw_scr[h] = (k_ref[:, qk(h)].astype(F32) * w_all[:, h:h + 1]).astype(BF16)

    for h in heads:
        pv_scr[h] = jnp.dot(p_scr[h], vx(h), preferred_element_type=F32)
        cu_scr[h] = lax.dot_general(kw_scr[h], vx(h), TN_DIMS, preferred_element_type=F32)

    for h in heads:
        inter = jnp.broadcast_to(inter_all[:, h:h + 1], (L, LANES))
        enmt = jnp.broadcast_to(enmt_all[:, h:h + 1], (L, LANES))
        num = pv_scr[h, :, 0:M_DV] + jnp.tile(inter, (1, M_DV // LANES)) * qc_scr[h, :, 0:M_DV]
        den = pv_scr[h, :, M_DV:M_DV + LANES] + inter * qc_scr[h, :, M_DV:M_DV + LANES]
        r = 1.0 / jnp.maximum(jnp.abs(den), enmt)
        ms = jnp.mean(num * num, axis=1, keepdims=True)
        scale = r * lax.rsqrt(r * r * ms + RMS_EPS)
        hn = (num * jnp.tile(scale, (1, M_DV // LANES)) * og_ref[:, dv(h)]).astype(BF16)
        y_scr[:, dv(h)] = (o_ref[:, dv(h)] * hn + cs_ref[:, dv(h)]) * sz_ref[:, dv(h)]
        if h == M_HEADS // 2 - 1:
            half = slice(0, M_INNER // 2)
            out_ref[...] = x_ref[...] + jnp.dot(y_scr[:, half], wout_ref[half, :],
                                                preferred_element_type=F32)
        if h == M_HEADS - 1:
            half = slice(M_INNER // 2, M_INNER)
            out_ref[...] += jnp.dot(y_scr[:, half], wout_ref[half, :], preferred_element_type=F32)

    for h in heads:
        ctx = decay[:, h:h + 1] * ctx_ref[h] + cu_scr[h]
        ctx_ref[h] = ctx
        ctxb_ref[h] = ctx.astype(BF16)


def _mlstm_rec(q, k, v, o, cs, sz, x, gc, gr, out_g, w_out):
    B, T, D = x.shape
    L = min(REC_L, T)
    nt = T // L
    dvx = M_DV + LANES
    full = lambda *shape: pl.BlockSpec(shape, lambda b, t: (0,) * len(shape))
    tok = lambda width: pl.BlockSpec((None, L, width), lambda b, t: (b, t, 0))
    return pl.pallas_call(
        _mlstm_rec_kernel,
        grid=(B, nt),
        in_specs=[
            tok(M_HEADS * M_DQK), tok(M_HEADS * M_DQK), tok(M_INNER), tok(M_INNER), tok(M_INNER),
            tok(M_INNER), tok(D), tok(GATE_PAD),
            pl.BlockSpec((None, 2 * M_HEADS, L), lambda b, t: (b, 0, t)),
            full(1, M_INNER), full(M_INNER, D),
        ],
        out_specs=tok(D),
        out_shape=jax.ShapeDtypeStruct((B, T, D), F32),
        scratch_shapes=[
            pltpu.VMEM((M_HEADS, M_DQK, dvx), F32),
            pltpu.VMEM((M_HEADS, M_DQK, dvx), BF16),
            pltpu.VMEM((1, LANES), F32),
            pltpu.VMEM((M_HEADS, L, L), F32),
            pltpu.VMEM((M_HEADS, L, dvx), F32),
            pltpu.VMEM((M_HEADS, L, L), BF16),
            pltpu.VMEM((M_HEADS, L, M_DQK), BF16),
            pltpu.VMEM((M_HEADS, L, dvx), F32),
            pltpu.VMEM((M_HEADS, M_DQK, dvx), F32),
            pltpu.VMEM((L, M_INNER), BF16),
        ],
        compiler_params=pltpu.CompilerParams(
            dimension_semantics=("parallel", "arbitrary"), vmem_limit_bytes=VMEM_LIMIT),
        name="mlstm_rec",
    )(q, k, v, o, cs, sz, x, gc, gr, out_g, w_out)


def _bias_kernel(tab_ref, out_ref):
    r = lax.broadcasted_iota(jnp.int32, (REL_SIZE, BIAS_W), 0)
    e = lax.broadcasted_iota(jnp.int32, (REL_SIZE, BIAS_W), 1)
    e = jnp.where(e >= BAND, e - BIAS_W, e)
    idx = jnp.clip(LEFT_CHUNKS * CHUNK - e, -REL_FUTURE, REL_PAST) + REL_FUTURE
    onehot = (r == idx).astype(F32)
    g = jnp.dot(tab_ref[...], onehot, precision=lax.Precision.HIGHEST,
                preferred_element_type=F32) * LOG2_E
    for j in range(A_PAIRS):
        halves = []
        for a in range(2):
            rows = jnp.broadcast_to(g[2 * j + a:2 * j + a + 1, :], (CHUNK, BIAS_W))
            halves.append(pltpu.roll(rows, 0, 1, stride=1, stride_axis=0))
        both = jnp.concatenate(halves, axis=0)
        out_ref[j] = both[:, 0:BAND]


def _attn_bias(rel_bias):
    return pl.pallas_call(
        _bias_kernel,
        out_shape=jax.ShapeDtypeStruct((A_PAIRS, 2 * CHUNK, BAND), F32),
        compiler_params=pltpu.CompilerParams(vmem_limit_bytes=VMEM_LIMIT),
        name="attn_bias",
    )(rel_bias)


def _attn_pre_kernel(x_ref, g_ref, win_ref, gain_ref, seg_ref, exp_ref,
                     q_ref, k_ref, v_ref, sz_ref):
    xn = (_rms(x_ref[...]) * g_ref[...]).astype(BF16)
    qk = jnp.dot(xn, win_ref[:, 0:2 * A_INNER], preferred_element_type=F32)
    ss = jnp.dot((qk * qk).astype(BF16), seg_ref[...], preferred_element_type=F32)
    inv = lax.rsqrt(ss * (1.0 / A_HD) + RMS_EPS)
    inv_hi = inv.astype(BF16)
    inv_lo = (inv - inv_hi.astype(F32)).astype(BF16)
    inv_e = jnp.dot(jnp.concatenate([inv_hi, inv_lo], axis=1), exp_ref[...],
                    preferred_element_type=F32)
    qkn = qk * inv_e * gain_ref[...]
    q_ref[...] = qkn[:, 0:A_INNER].astype(BF16)
    k_ref[...] = qkn[:, A_INNER:2 * A_INNER].astype(BF16)
    v_ref[...] = jnp.dot(xn, win_ref[:, 2 * A_INNER:3 * A_INNER],
                         preferred_element_type=F32).astype(BF16)
    z = jnp.dot(xn, win_ref[:, 3 * A_INNER:4 * A_INNER], preferred_element_type=F32)
    sz_ref[...] = (z * _sigmoid(z)).astype(BF16)


def _attn_pre(x, g, w_in, gain, seg, expand):
    B, T, D = x.shape
    bm = min(APRE_BM, T)
    nt = T // bm
    full = lambda *shape: pl.BlockSpec(shape, lambda b, t: (0,) * len(shape))
    tok = lambda width: pl.BlockSpec((None, bm, width), lambda b, t: (b, t, 0))
    o = jax.ShapeDtypeStruct((B, T, A_INNER), BF16)
    return pl.pallas_call(
        _attn_pre_kernel,
        grid=(B, nt),
        in_specs=[tok(D), full(1, D), full(D, 4 * A_INNER), full(1, 2 * A_INNER),
                  full(2 * A_INNER, LANES), full(2 * LANES, 2 * A_INNER)],
        out_specs=(tok(A_INNER),) * 4,
        out_shape=(o, o, o, o),
        compiler_params=pltpu.CompilerParams(
            dimension_semantics=("parallel", "parallel"), vmem_limit_bytes=VMEM_LIMIT),
        name="attn_pre",
    )(x, g, w_in, gain, seg, expand)


def _attn_kernel(q_ref, kp_ref, kc_ref, vp_ref, vc_ref, sz_ref, x_ref, bias_ref, wout_ref,
                 out_ref,
                 kcat, vcat, o_s, s_even, s_odd, m_even, m_odd):
    i = pl.program_id(1)
    bm = q_ref.shape[0]
    n_chunks = bm // CHUNK
    kcat[0:bm, :] = kp_ref[...]
    kcat[bm:2 * bm, :] = kc_ref[...]
    vcat[0:bm, :] = vp_ref[...]
    vcat[bm:2 * bm, :] = vc_ref[...]
    first_head = lax.broadcasted_iota(jnp.int32, (CHUNK, 2 * A_HD), 1) < A_HD
    key_off = lax.broadcasted_iota(jnp.int32, (1, BAND), 1)
    ones_tile = jnp.ones((BAND, LANES), BF16)

    lanes = lambda j: slice(j * 2 * A_HD, (j + 1) * 2 * A_HD)

    def scores(masked, c, j, s_ref, m_ref):
        r0 = pl.multiple_of(c * CHUNK, CHUNK)
        q2 = q_ref[pl.ds(r0, CHUNK), lanes(j)].astype(F32)
        wt = jnp.concatenate([jnp.where(first_head, q2, 0.0),
                              jnp.where(first_head, 0.0, q2)], axis=0).astype(BF16)
        k2 = kcat[pl.ds(r0 + bm - LEFT_CHUNKS * CHUNK, BAND), lanes(j)]
        s = lax.dot_general(wt, k2, NT_DIMS, preferred_element_type=F32) + bias_ref[j]
        if masked:
            valid = (c * CHUNK - LEFT_CHUNKS * CHUNK + key_off) >= 0
            s = jnp.where(valid, s, -jnp.inf)
        s_ref[j] = s
        m_ref[j] = jnp.broadcast_to(jnp.max(s, axis=1, keepdims=True), (2 * CHUNK, LANES))

    def finish(c, j, s_ref, m_ref):
        r0 = pl.multiple_of(c * CHUNK, CHUNK)
        m = m_ref[j]
        m_band = jnp.concatenate([m] * (BAND // LANES) + [m[:, 0:BAND % LANES]], axis=1)
        e = jnp.exp2(s_ref[j] - m_band)
        v2 = jnp.concatenate([vcat[pl.ds(r0 + bm - LEFT_CHUNKS * CHUNK, BAND), lanes(j)], ones_tile],
                             axis=1)
        r = jnp.dot(e.astype(BF16), v2, preferred_element_type=F32)
        r = r[:, 0:2 * A_HD] * (1.0 / r[:, 2 * A_HD:4 * A_HD])
        o2 = jnp.where(first_head, r[0:CHUNK, :], r[CHUNK:2 * CHUNK, :])
        o_s[pl.ds(r0, CHUNK), lanes(j)] = (
            o2 * sz_ref[pl.ds(r0, CHUNK), lanes(j)].astype(F32)).astype(BF16)

    even, odd = (s_even, m_even), (s_odd, m_odd)

    def chunk(masked, c, cur, nxt):
        for j in range(A_PAIRS):
            finish(c, j, *cur)
            if nxt is not None:
                scores(masked, c + 1, j, *nxt)

    def run(masked):
        for j in range(A_PAIRS):
            scores(masked, 0, j, *even)

        def two_chunks(cc, carry):
            chunk(masked, 2 * cc, even, odd)
            chunk(masked, 2 * cc + 1, odd, even)
            return carry

        lax.fori_loop(0, n_chunks // 2 - 1, two_chunks, 0)
        chunk(masked, n_chunks - 2, even, odd)
        chunk(masked, n_chunks - 1, odd, None)

    @pl.when(i == 0)
    def _():
        run(True)

    @pl.when(i > 0)
    def _():
        run(False)

    out_ref[...] = x_ref[...] + jnp.dot(o_s[...], wout_ref[...], preferred_element_type=F32)


def _attn(q, k, v, sz, x, bias, w_out):
    B, T, D = x.shape
    bm = ATT_BM
    nt = T // bm
    full = lambda *shape: pl.BlockSpec(shape, lambda b, t: (0,) * len(shape))
    cur = lambda width: pl.BlockSpec((None, bm, width), lambda b, t: (b, t, 0))
    prev = lambda width: pl.BlockSpec((None, bm, width), lambda b, t: (b, jnp.maximum(t - 1, 0), 0))
    return pl.pallas_call(
        _attn_kernel,
        grid=(B, nt),
        in_specs=[cur(A_INNER), prev(A_INNER), cur(A_INNER), prev(A_INNER), cur(A_INNER),
                  cur(A_INNER), cur(D), full(A_PAIRS, 2 * CHUNK, BAND), full(A_INNER, D)],
        out_specs=cur(D),
        out_shape=jax.ShapeDtypeStruct((B, T, D), F32),
        scratch_shapes=[
            pltpu.VMEM((2 * bm, A_INNER), BF16),
            pltpu.VMEM((2 * bm, A_INNER), BF16),
            pltpu.VMEM((bm, A_INNER), BF16),
            pltpu.VMEM((A_PAIRS, 2 * CHUNK, BAND), F32),
            pltpu.VMEM((A_PAIRS, 2 * CHUNK, BAND), F32),
            pltpu.VMEM((A_PAIRS, 2 * CHUNK, LANES), F32),
            pltpu.VMEM((A_PAIRS, 2 * CHUNK, LANES), F32),
        ],
        compiler_params=pltpu.CompilerParams(
            dimension_semantics=("parallel", "parallel"), vmem_limit_bytes=VMEM_LIMIT),
        name="attn",
    )(q, k, k, v, v, sz, x, bias, w_out)


def _qk_norm_constants():
    col = jnp.arange(2 * A_INNER) // A_HD
    seg = (col[:, None] == jnp.arange(LANES)[None, :]).astype(BF16)
    row = jnp.arange(2 * LANES) % LANES
    expand = (row[:, None] == col[None, :]).astype(BF16)
    return seg, expand


def kernel(x, norm_g, a_w_in, a_conv_w, a_conv_b, a_w_q, a_w_k, a_w_v, a_w_o, a_w_if, a_b_if,
           a_out_g, a_skip, a_w_out, b_w_in, b_q_g, b_k_g, b_rel_bias, b_w_out):
    assert x.shape[1] % ATT_BM == 0 and x.shape[2] == D_MODEL
    bf = lambda w: w.astype(BF16)
    pad_gate = lambda w: jnp.pad(w, ((0, 0), (0, GATE_PAD - 2 * M_HEADS)))

    q, k, v, o, cs, sz, gc, gr = _mlstm_pre(
        x, norm_g[0][None, :], bf(a_w_in[0]), a_conv_w[0], a_conv_b[0][None, :],
        bf(a_w_q[0]), bf(a_w_k[0]), bf(a_w_v[0]), bf(a_w_o[0]),
        bf(pad_gate(a_w_if[0])), pad_gate(a_b_if[0][None, :]), a_skip[0][None, :])
    x1 = _mlstm_rec(q, k, v, o, cs, sz, x, gc, gr, a_out_g[0].reshape(1, M_INNER), bf(a_w_out[0]))

    bias = _attn_bias(b_rel_bias[0])
    gain = jnp.concatenate([jnp.tile(b_q_g[0], A_HEADS) * (A_HD ** -0.5 * LOG2_E),
                            jnp.tile(b_k_g[0], A_HEADS)])[None, :]
    seg, expand = _qk_norm_constants()
    q, k, v, sz = _attn_pre(x1, norm_g[1][None, :], bf(b_w_in[0]), gain, seg, expand)
    return _attn(q, k, v, sz, x1, bias, bf(b_w_out[0]))
```

```python
import functools

import jax
import jax.numpy as jnp
from jax import lax
from jax.experimental import pallas as pl
from jax.experimental.pallas import tpu as pltpu

F32 = jnp.float32
BF16 = jnp.bfloat16

RMS_EPS = 1e-6
CHUNK = 64
LOG2_E = 1.4426950408889634

D_MODEL = 1024
M_INNER = 2 * D_MODEL
M_HEADS = 4
M_DV = M_INNER // M_HEADS
M_DQK = M_DV // 2
CONV_K = 4

A_INNER = D_MODEL
A_HEADS = 16
A_HD = A_INNER // A_HEADS
A_PAIRS = A_HEADS // 2
LEFT_CHUNKS = 8
BAND = (LEFT_CHUNKS + 1) * CHUNK
REL_PAST = 256
REL_FUTURE = CHUNK - 1
REL_SIZE = REL_PAST + REL_FUTURE + 1

LANES = 128
SUBLANES = 8
GATE_PAD = LANES
VMEM_LIMIT = 56 * 1024 * 1024
PRE_BM = 256
REC_L = 256
APRE_BM = 256
ATT_BM = LEFT_CHUNKS * CHUNK
BIAS_W = 640

NT_DIMS = (((1,), (1,)), ((), ()))
TN_DIMS = (((0,), (0,)), ((), ()))


def _sigmoid(x):
    return 1.0 / (1.0 + jnp.exp(-x))


def _log_sigmoid(x):
    return jnp.minimum(x, 0.0) - jnp.log1p(jnp.exp(-jnp.abs(x)))


def _rms(x, eps=RMS_EPS):
    return x * lax.rsqrt(jnp.mean(x * x, axis=-1, keepdims=True) + eps)


def _mlstm_pre_kernel(x_ref, g_ref, win_ref, cw_ref, cb_ref, wq_ref, wk_ref, wv_ref, wo_ref,
                      wif_ref, bif_ref, skip_ref,
                      q_ref, k_ref, v_ref, o_ref, cs_ref, sz_ref, gc_ref, gr_ref,
                      ubuf, cb_scr):
    t = pl.program_id(1)
    bm = x_ref.shape[0]
    dv = lambda h: slice(h * M_DV, (h + 1) * M_DV)
    qk = lambda h: slice(h * M_DQK, (h + 1) * M_DQK)

    @pl.when(t == 0)
    def _():
        ubuf[0:SUBLANES, :] = jnp.zeros((SUBLANES, M_INNER), F32)

    xn = (_rms(x_ref[...]) * g_ref[...]).astype(BF16)

    for h in range(M_HEADS):
        ubuf[SUBLANES:SUBLANES + bm, dv(h)] = jnp.dot(xn, win_ref[:, dv(h)],
                                                      preferred_element_type=F32)

    def gate_branch(h):
        z = jnp.dot(xn, win_ref[:, M_INNER + h * M_DV:M_INNER + (h + 1) * M_DV],
                    preferred_element_type=F32)
        sz_ref[:, dv(h)] = (z * _sigmoid(z)).astype(BF16)

    def conv_branch(h):
        u = ubuf[SUBLANES:SUBLANES + bm, dv(h)]
        conv = cb_ref[:, dv(h)] + cw_ref[CONV_K - 1:CONV_K, dv(h)] * u
        for j in range(CONV_K - 1):
            shifted = ubuf[pl.ds(SUBLANES - (CONV_K - 1) + j, bm), dv(h)]
            conv = conv + cw_ref[j:j + 1, dv(h)] * shifted
        c = conv * _sigmoid(conv)
        cb_scr[:, dv(h)] = c.astype(BF16)
        cs_ref[:, dv(h)] = (skip_ref[:, dv(h)] * c).astype(BF16)

    def value_projections(h):
        ub16 = ubuf[SUBLANES:SUBLANES + bm, dv(h)].astype(BF16)
        v_ref[:, dv(h)] = jnp.dot(ub16, wv_ref[h], preferred_element_type=F32).astype(BF16)
        o_ref[:, dv(h)] = _sigmoid(jnp.dot(ub16, wo_ref[h], preferred_element_type=F32)).astype(BF16)

    def key_projections(h):
        cb16 = cb_scr[:, dv(h)]
        q_ref[:, qk(h)] = jnp.dot(cb16, wq_ref[h], preferred_element_type=F32).astype(BF16)
        k_ref[:, qk(h)] = (jnp.dot(cb16, wk_ref[h], preferred_element_type=F32)
                           * (M_DQK ** -0.5)).astype(BF16)

    for h in range(M_HEADS):
        conv_branch(h)
        value_projections(h)
        if h > 0:
            key_projections(h - 1)
    ubuf[0:SUBLANES, :] = ubuf[bm:bm + SUBLANES, :]

    gates = bif_ref[...] + jnp.dot(cb_scr[...], wif_ref[...], preferred_element_type=F32)

    lf = _log_sigmoid(gates)
    lf_hi = lf.astype(BF16)
    rem = lf - lf_hi.astype(F32)
    lf_mid = rem.astype(BF16)
    lf_lo = (rem - lf_mid.astype(F32)).astype(BF16)
    tril = (lax.broadcasted_iota(jnp.int32, (bm, bm), 1)
            <= lax.broadcasted_iota(jnp.int32, (bm, bm), 0)).astype(BF16)
    cum = (jnp.dot(tril, lf_hi, preferred_element_type=F32)
           + jnp.dot(tril, lf_mid, preferred_element_type=F32)
           + jnp.dot(tril, lf_lo, preferred_element_type=F32))
    b = pltpu.roll(cum, GATE_PAD - M_HEADS, 1)
    ar = gates - b
    row = lax.broadcasted_iota(jnp.int32, (bm, GATE_PAD), 0)
    cm = ar
    shift = 1
    while shift < bm:
        cm = jnp.maximum(cm, jnp.where(row >= shift, pltpu.roll(cm, shift, 0), -jnp.inf))
        shift *= 2
    lane = lax.broadcasted_iota(jnp.int32, (bm, GATE_PAD), 1)
    gc_ref[...] = jnp.where(lane < M_HEADS, b,
                            jnp.where(lane < 2 * M_HEADS, pltpu.roll(ar, M_HEADS, 1),
                                      pltpu.roll(cm, 2 * M_HEADS, 1)))
    gr_ref[...] = ar.T[0:2 * M_HEADS, :]

    for h in range(M_HEADS):
        gate_branch(h)
    key_projections(M_HEADS - 1)


def _mlstm_pre(x, g, w_in, conv_w, conv_b, w_q, w_k, w_v, w_o, w_if, b_if, skip):
    B, T, D = x.shape
    bm = min(PRE_BM, T)
    nt = T // bm
    full = lambda *shape: pl.BlockSpec(shape, lambda b, t: (0,) * len(shape))
    tok = lambda width: pl.BlockSpec((None, bm, width), lambda b, t: (b, t, 0))
    out_shape = (
        jax.ShapeDtypeStruct((B, T, M_HEADS * M_DQK), BF16),
        jax.ShapeDtypeStruct((B, T, M_HEADS * M_DQK), BF16),
        jax.ShapeDtypeStruct((B, T, M_INNER), BF16),
        jax.ShapeDtypeStruct((B, T, M_INNER), BF16),
        jax.ShapeDtypeStruct((B, T, M_INNER), BF16),
        jax.ShapeDtypeStruct((B, T, M_INNER), BF16),
        jax.ShapeDtypeStruct((B, T, GATE_PAD), F32),
        jax.ShapeDtypeStruct((B, 2 * M_HEADS, T), F32),
    )
    return pl.pallas_call(
        _mlstm_pre_kernel,
        grid=(B, nt),
        in_specs=[
            tok(D), full(1, D), full(D, 2 * M_INNER), full(CONV_K, M_INNER), full(1, M_INNER),
            full(M_HEADS, M_DV, M_DQK), full(M_HEADS, M_DV, M_DQK),
            full(M_HEADS, M_DV, M_DV), full(M_HEADS, M_DV, M_DV),
            full(M_INNER, GATE_PAD), full(1, GATE_PAD), full(1, M_INNER),
        ],
        out_specs=(
            tok(M_HEADS * M_DQK), tok(M_HEADS * M_DQK), tok(M_INNER), tok(M_INNER), tok(M_INNER),
            tok(M_INNER), tok(GATE_PAD),
            pl.BlockSpec((None, 2 * M_HEADS, bm), lambda b, t: (b, 0, t)),
        ),
        out_shape=out_shape,
        scratch_shapes=[pltpu.VMEM((SUBLANES + bm, M_INNER), F32),
                        pltpu.VMEM((bm, M_INNER), BF16)],
        compiler_params=pltpu.CompilerParams(
            dimension_semantics=("parallel", "arbitrary"), vmem_limit_bytes=VMEM_LIMIT),
        name="mlstm_pre",
    )(x, g, w_in, conv_w, conv_b, w_q, w_k, w_v, w_o, w_if, b_if, skip)


def _mlstm_rec_kernel(q_ref, k_ref, v_ref, o_ref, cs_ref, sz_ref, x_ref, gc_ref, gr_ref,
                      og_ref, wout_ref,
                      out_ref,
                      ctx_ref, ctxb_ref, m_ref, s_scr, qc_scr, p_scr, kw_scr, pv_scr, cu_scr, y_scr):
    t = pl.program_id(1)
    L = q_ref.shape[0]

    @pl.when(t == 0)
    def _():
        ctx_ref[...] = jnp.zeros(ctx_ref.shape, F32)
        ctxb_ref[...] = jnp.zeros(ctxb_ref.shape, BF16)
        m_ref[...] = jnp.zeros(m_ref.shape, F32)

    heads = range(M_HEADS)
    qk = lambda h: slice(h * M_DQK, (h + 1) * M_DQK)
    dv = lambda h: slice(h * M_DV, (h + 1) * M_DV)
    ones_tile = jnp.ones((L, LANES), BF16)
    vx = lambda h: jnp.concatenate([v_ref[:, dv(h)], ones_tile], axis=1)

    for h in heads:
        qh = q_ref[:, qk(h)]
        s_scr[h] = lax.dot_general(qh, k_ref[:, qk(h)], NT_DIMS, preferred_element_type=F32)
        qc_scr[h] = jnp.dot(qh, ctxb_ref[h], preferred_element_type=F32)

    gc = gc_ref[...]
    b = gc
    ar = pltpu.roll(gc, GATE_PAD - M_HEADS, 1)
    cm = pltpu.roll(gc, GATE_PAD - 2 * M_HEADS, 1)
    m_prev = m_ref[...]
    g_last = b[L - 1:L, :]
    a_c = g_last + ar
    m_new = jnp.maximum(g_last + m_prev, jnp.max(a_c, axis=0, keepdims=True))
    w_all = jnp.exp(a_c - m_new)
    decay = jnp.exp(g_last + m_prev - m_new)
    mx = jnp.maximum(m_prev, cm)
    inter_all = jnp.exp(m_prev - mx)
    enmt_all = jnp.exp(-(b + mx))
    m_ref[...] = m_new
    gr = gr_ref[...]
    tril = (lax.broadcasted_iota(jnp.int32, (L, L), 1)
            <= lax.broadcasted_iota(jnp.int32, (L, L), 0))

    for h in heads:
        d = jnp.where(tril, jnp.exp(gr[h:h + 1, :] - mx[:, h:h + 1]), 0.0)
        p_scr[h] = (s_scr[h] * d).astype(BF16)
        kw_scr[h] = (k_ref[:, qk(h)].astype(F32) * w_all[:, h:h + 1]).astype(BF16)

    for h in heads:
        pv_scr[h] = jnp.dot(p_scr[h], vx(h), preferred_element_type=F32)
        cu_scr[h] = lax.dot_general(kw_scr[h], vx(h), TN_DIMS, preferred_element_type=F32)

    for h in heads:
        inter = jnp.broadcast_to(inter_all[:, h:h + 1], (L, LANES))
        enmt = jnp.broadcast_to(enmt_all[:, h:h + 1], (L, LANES))
        num = pv_scr[h, :, 0:M_DV] + jnp.tile(inter, (1, M_DV // LANES)) * qc_scr[h, :, 0:M_DV]
        den = pv_scr[h, :, M_DV:M_DV + LANES] + inter * qc_scr[h, :, M_DV:M_DV + LANES]
        r = 1.0 / jnp.maximum(jnp.abs(den), enmt)
        ms = jnp.mean(num * num, axis=1, keepdims=True)
        scale = r * lax.rsqrt(r * r * ms + RMS_EPS)
        hn = (num * jnp.tile(scale, (1, M_DV // LANES)) * og_ref[:, dv(h)]).astype(BF16)
        y_scr[:, dv(h)] = (o_ref[:, dv(h)] * hn + cs_ref[:, dv(h)]) * sz_ref[:, dv(h)]
        if h == M_HEADS // 2 - 1:
            half = slice(0, M_INNER // 2)
            out_ref[...] = x_ref[...] + jnp.dot(y_scr[:, half], wout_ref[half, :],
                                                preferred_element_type=F32)
        if h == M_HEADS - 1:
            half = slice(M_INNER // 2, M_INNER)
            out_ref[...] += jnp.dot(y_scr[:, half], wout_ref[half, :], preferred_element_type=F32)

    for h in heads:
        ctx = decay[:, h:h + 1] * ctx_ref[h] + cu_scr[h]
        ctx_ref[h] = ctx
        ctxb_ref[h] = ctx.astype(BF16)


def _mlstm_rec(q, k, v, o, cs, sz, x, gc, gr, out_g, w_out):
    B, T, D = x.shape
    L = min(REC_L, T)
    nt = T // L
    dvx = M_DV + LANES
    full = lambda *shape: pl.BlockSpec(shape, lambda b, t: (0,) * len(shape))
    tok = lambda width: pl.BlockSpec((None, L, width), lambda b, t: (b, t, 0))
    return pl.pallas_call(
        _mlstm_rec_kernel,
        grid=(B, nt),
        in_specs=[
            tok(M_HEADS * M_DQK), tok(M_HEADS * M_DQK), tok(M_INNER), tok(M_INNER), tok(M_INNER),
            tok(M_INNER), tok(D), tok(GATE_PAD),
            pl.BlockSpec((None, 2 * M_HEADS, L), lambda b, t: (b, 0, t)),
            full(1, M_INNER), full(M_INNER, D),
        ],
        out_specs=tok(D),
        out_shape=jax.ShapeDtypeStruct((B, T, D), F32),
        scratch_shapes=[
            pltpu.VMEM((M_HEADS, M_DQK, dvx), F32),
            pltpu.VMEM((M_HEADS, M_DQK, dvx), BF16),
            pltpu.VMEM((1, LANES), F32),
            pltpu.VMEM((M_HEADS, L, L), F32),
            pltpu.VMEM((M_HEADS, L, dvx), F32),
            pltpu.VMEM((M_HEADS, L, L), BF16),
            pltpu.VMEM((M_HEADS, L, M_DQK), BF16),
            pltpu.VMEM((M_HEADS, L, dvx), F32),
            pltpu.VMEM((M_HEADS, M_DQK, dvx), F32),
            pltpu.VMEM((L, M_INNER), BF16),
        ],
        compiler_params=pltpu.CompilerParams(
            dimension_semantics=("parallel", "arbitrary"), vmem_limit_bytes=VMEM_LIMIT),
        name="mlstm_rec",
    )(q, k, v, o, cs, sz, x, gc, gr, out_g, w_out)


def _bias_kernel(tab_ref, out_ref):
    r = lax.broadcasted_iota(jnp.int32, (REL_SIZE, BIAS_W), 0)
    e = lax.broadcasted_iota(jnp.int32, (REL_SIZE, BIAS_W), 1)
    e = jnp.where(e >= BAND, e - BIAS_W, e)
    idx = jnp.clip(LEFT_CHUNKS * CHUNK - e, -REL_FUTURE, REL_PAST) + REL_FUTURE
    onehot = (r == idx).astype(F32)
    g = jnp.dot(tab_ref[...], onehot, precision=lax.Precision.HIGHEST,
                preferred_element_type=F32) * LOG2_E
    for j in range(A_PAIRS):
        halves = []
        for a in range(2):
            rows = jnp.broadcast_to(g[2 * j + a:2 * j + a + 1, :], (CHUNK, BIAS_W))
            halves.append(pltpu.roll(rows, 0, 1, stride=1, stride_axis=0))
        both = jnp.concatenate(halves, axis=0)
        out_ref[j] = both[:, 0:BAND]


def _attn_bias(rel_bias):
    return pl.pallas_call(
        _bias_kernel,
        out_shape=jax.ShapeDtypeStruct((A_PAIRS, 2 * CHUNK, BAND), F32),
        compiler_params=pltpu.CompilerParams(vmem_limit_bytes=VMEM_LIMIT),
        name="attn_bias",
    )(rel_bias)


def _attn_pre_kernel(x_ref, g_ref, win_ref, gain_ref,
                     q_ref, k_ref, v_ref, sz_ref,
                     qk_scr):
    bm = x_ref.shape[0]
    xn = (_rms(x_ref[...]) * g_ref[...]).astype(BF16)
    qk_scr[...] = jnp.dot(xn, win_ref[:, 0:2 * A_INNER], preferred_element_type=F32)
    v_ref[...] = jnp.dot(xn, win_ref[:, 2 * A_INNER:3 * A_INNER],
                         preferred_element_type=F32).astype(BF16)
    z = jnp.dot(xn, win_ref[:, 3 * A_INNER:4 * A_INNER], preferred_element_type=F32)
    sz_ref[...] = (z * _sigmoid(z)).astype(BF16)

    first_head = lax.broadcasted_iota(jnp.int32, (bm, 2 * A_HD), 1) < A_HD
    for j in range(2 * A_PAIRS):
        lanes = slice(j * 2 * A_HD, (j + 1) * 2 * A_HD)
        xt = qk_scr[:, lanes]
        sq = xt * xt
        both = jnp.sum(sq, axis=1, keepdims=True)
        head_a = jnp.sum(jnp.where(first_head, sq, 0.0), axis=1, keepdims=True)
        inv_a = lax.rsqrt(head_a * (1.0 / A_HD) + RMS_EPS)
        inv_b = lax.rsqrt((both - head_a) * (1.0 / A_HD) + RMS_EPS)
        out = (xt * jnp.where(first_head, inv_a, inv_b) * gain_ref[:, lanes]).astype(BF16)
        if j < A_PAIRS:
            q_ref[:, lanes] = out
        else:
            k_ref[:, j * 2 * A_HD - A_INNER:(j + 1) * 2 * A_HD - A_INNER] = out


def _attn_pre(x, g, w_in, gain):
    B, T, D = x.shape
    bm = min(APRE_BM, T)
    nt = T // bm
    full = lambda *shape: pl.BlockSpec(shape, lambda b, t: (0,) * len(shape))
    tok = lambda width: pl.BlockSpec((None, bm, width), lambda b, t: (b, t, 0))
    o = jax.ShapeDtypeStruct((B, T, A_INNER), BF16)
    return pl.pallas_call(
        _attn_pre_kernel,
        grid=(B, nt),
        in_specs=[tok(D), full(1, D), full(D, 4 * A_INNER), full(1, 2 * A_INNER)],
        out_specs=(tok(A_INNER),) * 4,
        out_shape=(o, o, o, o),
        scratch_shapes=[pltpu.VMEM((bm, 2 * A_INNER), F32)],
        compiler_params=pltpu.CompilerParams(
            dimension_semantics=("parallel", "parallel"), vmem_limit_bytes=VMEM_LIMIT),
        name="attn_pre",
    )(x, g, w_in, gain)


def _attn_kernel(q_ref, kp_ref, kc_ref, vp_ref, vc_ref, sz_ref, x_ref, bias_ref, wout_ref,
                 out_ref,
                 kcat, vcat, o_s, s_even, s_odd, m_even, m_odd):
    i = pl.program_id(1)
    bm = q_ref.shape[0]
    n_chunks = bm // CHUNK
    kcat[0:bm, :] = kp_ref[...]
    kcat[bm:2 * bm, :] = kc_ref[...]
    vcat[0:bm, :] = vp_ref[...]
    vcat[bm:2 * bm, :] = vc_ref[...]
    first_head = lax.broadcasted_iota(jnp.int32, (CHUNK, 2 * A_HD), 1) < A_HD
    key_off = lax.broadcasted_iota(jnp.int32, (1, BAND), 1)
    ones_tile = jnp.ones((BAND, LANES), BF16)

    lanes = lambda j: slice(j * 2 * A_HD, (j + 1) * 2 * A_HD)

    def scores(masked, c, j, s_ref, m_ref):
        r0 = pl.multiple_of(c * CHUNK, CHUNK)
        q2 = q_ref[pl.ds(r0, CHUNK), lanes(j)].astype(F32)
        wt = jnp.concatenate([jnp.where(first_head, q2, 0.0),
                              jnp.where(first_head, 0.0, q2)], axis=0).astype(BF16)
        k2 = kcat[pl.ds(r0 + bm - LEFT_CHUNKS * CHUNK, BAND), lanes(j)]
        s = lax.dot_general(wt, k2, NT_DIMS, preferred_element_type=F32) + bias_ref[j]
        if masked:
            valid = (c * CHUNK - LEFT_CHUNKS * CHUNK + key_off) >= 0
            s = jnp.where(valid, s, -jnp.inf)
        s_ref[j] = s
        m_ref[j] = jnp.broadcast_to(jnp.max(s, axis=1, keepdims=True), (2 * CHUNK, LANES))

    def finish(c, j, s_ref, m_ref):
        r0 = pl.multiple_of(c * CHUNK, CHUNK)
        m = m_ref[j]
        m_band = jnp.concatenate([m] * (BAND // LANES) + [m[:, 0:BAND % LANES]], axis=1)
        e = jnp.exp2(s_ref[j] - m_band)
        v2 = jnp.concatenate([vcat[pl.ds(r0 + bm - LEFT_CHUNKS * CHUNK, BAND), lanes(j)], ones_tile],
                             axis=1)
        r = jnp.dot(e.astype(BF16), v2, preferred_element_type=F32)
        r = r[:, 0:2 * A_HD] * (1.0 / r[:, 2 * A_HD:4 * A_HD])
        o2 = jnp.where(first_head, r[0:CHUNK, :], r[CHUNK:2 * CHUNK, :])
        o_s[pl.ds(r0, CHUNK), lanes(j)] = (
            o2 * sz_ref[pl.ds(r0, CHUNK), lanes(j)].astype(F32)).astype(BF16)

    even, odd = (s_even, m_even), (s_odd, m_odd)

    def chunk(masked, c, cur, nxt):
        for j in range(A_PAIRS):
            finish(c, j, *cur)
            if nxt is not None:
                scores(masked, c + 1, j, *nxt)

    def run(masked):
        for j in range(A_PAIRS):
            scores(masked, 0, j, *even)

        def two_chunks(cc, carry):
            chunk(masked, 2 * cc, even, odd)
            chunk(masked, 2 * cc + 1, odd, even)
            return carry

        lax.fori_loop(0, n_chunks // 2 - 1, two_chunks, 0)
        chunk(masked, n_chunks - 2, even, odd)
        chunk(masked, n_chunks - 1, odd, None)

    @pl.when(i == 0)
    def _():
        run(True)

    @pl.when(i > 0)
    def _():
        run(False)

    out_ref[...] = x_ref[...] + jnp.dot(o_s[...], wout_ref[...], preferred_element_type=F32)


def _attn(q, k, v, sz, x, bias, w_out):
    B, T, D = x.shape
    bm = ATT_BM
    nt = T // bm
    full = lambda *shape: pl.BlockSpec(shape, lambda b, t: (0,) * len(shape))
    cur = lambda width: pl.BlockSpec((None, bm, width), lambda b, t: (b, t, 0))
    prev = lambda width: pl.BlockSpec((None, bm, width), lambda b, t: (b, jnp.maximum(t - 1, 0), 0))
    return pl.pallas_call(
        _attn_kernel,
        grid=(B, nt),
        in_specs=[cur(A_INNER), prev(A_INNER), cur(A_INNER), prev(A_INNER), cur(A_INNER),
                  cur(A_INNER), cur(D), full(A_PAIRS, 2 * CHUNK, BAND), full(A_INNER, D)],
        out_specs=cur(D),
        out_shape=jax.ShapeDtypeStruct((B, T, D), F32),
        scratch_shapes=[
            pltpu.VMEM((2 * bm, A_INNER), BF16),
            pltpu.VMEM((2 * bm, A_INNER), BF16),
            pltpu.VMEM((bm, A_INNER), BF16),
            pltpu.VMEM((A_PAIRS, 2 * CHUNK, BAND), F32),
            pltpu.VMEM((A_PAIRS, 2 * CHUNK, BAND), F32),
            pltpu.VMEM((A_PAIRS, 2 * CHUNK, LANES), F32),
            pltpu.VMEM((A_PAIRS, 2 * CHUNK, LANES), F32),
        ],
        compiler_params=pltpu.CompilerParams(
            dimension_semantics=("parallel", "parallel"), vmem_limit_bytes=VMEM_LIMIT),
        name="attn",
    )(q, k, k, v, v, sz, x, bias, w_out)


def kernel(x, norm_g, a_w_in, a_conv_w, a_conv_b, a_w_q, a_w_k, a_w_v, a_w_o, a_w_if, a_b_if,
           a_out_g, a_skip, a_w_out, b_w_in, b_q_g, b_k_g, b_rel_bias, b_w_out):
    assert x.shape[1] % ATT_BM == 0 and x.shape[2] == D_MODEL
    bf = lambda w: w.astype(BF16)
    pad_gate = lambda w: jnp.pad(w, ((0, 0), (0, GATE_PAD - 2 * M_HEADS)))

    q, k, v, o, cs, sz, gc, gr = _mlstm_pre(
        x, norm_g[0][None, :], bf(a_w_in[0]), a_conv_w[0], a_conv_b[0][None, :],
        bf(a_w_q[0]), bf(a_w_k[0]), bf(a_w_v[0]), bf(a_w_o[0]),
        bf(pad_gate(a_w_if[0])), pad_gate(a_b_if[0][None, :]), a_skip[0][None, :])
    x1 = _mlstm_rec(q, k, v, o, cs, sz, x, gc, gr, a_out_g[0].reshape(1, M_INNER), bf(a_w_out[0]))

    bias = _attn_bias(b_rel_bias[0])
    gain = jnp.concatenate([jnp.tile(b_q_g[0], A_HEADS) * (A_HD ** -0.5 * LOG2_E),
                            jnp.tile(b_k_g[0], A_HEADS)])[None, :]
    q, k, v, sz = _attn_pre(x1, norm_g[1][None, :], bf(b_w_in[0]), gain)
    return _attn(q, k, v, sz, x1, bias, bf(b_w_out[0]))
```

```python
import functools

import jax
import jax.numpy as jnp
from jax import lax
from jax.experimental import pallas as pl
from jax.experimental.pallas import tpu as pltpu

F32 = jnp.float32
BF16 = jnp.bfloat16

RMS_EPS = 1e-6
CHUNK = 64
LOG2_E = 1.4426950408889634

D_MODEL = 1024
M_INNER = 2 * D_MODEL
M_HEADS = 4
M_DV = M_INNER // M_HEADS
M_DQK = M_DV // 2
CONV_K = 4

A_INNER = D_MODEL
A_HEADS = 16
A_HD = A_INNER // A_HEADS
A_PAIRS = A_HEADS // 2
LEFT_CHUNKS = 8
BAND = (LEFT_CHUNKS + 1) * CHUNK
REL_PAST = 256
REL_FUTURE = CHUNK - 1
REL_SIZE = REL_PAST + REL_FUTURE + 1

LANES = 128
SUBLANES = 8
GATE_PAD = LANES
VMEM_LIMIT = 56 * 1024 * 1024
PRE_BM = 256
REC_L = 256
APRE_BM = 256
ATT_BM = LEFT_CHUNKS * CHUNK
BAND2 = BAND + CHUNK
BIAS_W = BAND2

NT_DIMS = (((1,), (1,)), ((), ()))
TN_DIMS = (((0,), (0,)), ((), ()))


def _sigmoid(x):
    return 1.0 / (1.0 + jnp.exp(-x))


def _log_sigmoid(x):
    return jnp.minimum(x, 0.0) - jnp.log1p(jnp.exp(-jnp.abs(x)))


def _rms(x, eps=RMS_EPS):
    return x * lax.rsqrt(jnp.mean(x * x, axis=-1, keepdims=True) + eps)


def _mlstm_pre_kernel(x_ref, g_ref, win_ref, cw_ref, cb_ref, wq_ref, wk_ref, wv_ref, wo_ref,
                      wif_ref, bif_ref, skip_ref,
                      q_ref, k_ref, v_ref, o_ref, cs_ref, sz_ref, gc_ref, gr_ref,
                      ubuf, cb_scr):
    t = pl.program_id(1)
    bm = x_ref.shape[0]
    dv = lambda h: slice(h * M_DV, (h + 1) * M_DV)
    qk = lambda h: slice(h * M_DQK, (h + 1) * M_DQK)

    @pl.when(t == 0)
    def _():
        ubuf[0:SUBLANES, :] = jnp.zeros((SUBLANES, M_INNER), F32)

    xn = (_rms(x_ref[...]) * g_ref[...]).astype(BF16)

    for h in range(M_HEADS):
        ubuf[SUBLANES:SUBLANES + bm, dv(h)] = jnp.dot(xn, win_ref[:, dv(h)],
                                                      preferred_element_type=F32)

    def gate_branch(h):
        z = jnp.dot(xn, win_ref[:, M_INNER + h * M_DV:M_INNER + (h + 1) * M_DV],
                    preferred_element_type=F32)
        sz_ref[:, dv(h)] = (z * _sigmoid(z)).astype(BF16)

    def conv_branch(h):
        u = ubuf[SUBLANES:SUBLANES + bm, dv(h)]
        conv = cb_ref[:, dv(h)] + cw_ref[CONV_K - 1:CONV_K, dv(h)] * u
        for j in range(CONV_K - 1):
            shifted = ubuf[pl.ds(SUBLANES - (CONV_K - 1) + j, bm), dv(h)]
            conv = conv + cw_ref[j:j + 1, dv(h)] * shifted
        c = conv * _sigmoid(conv)
        cb_scr[:, dv(h)] = c.astype(BF16)
        cs_ref[:, dv(h)] = (skip_ref[:, dv(h)] * c).astype(BF16)

    def value_projections(h):
        ub16 = ubuf[SUBLANES:SUBLANES + bm, dv(h)].astype(BF16)
        v_ref[:, dv(h)] = jnp.dot(ub16, wv_ref[h], preferred_element_type=F32).astype(BF16)
        o_ref[:, dv(h)] = _sigmoid(jnp.dot(ub16, wo_ref[h], preferred_element_type=F32)).astype(BF16)

    def key_projections(h):
        cb16 = cb_scr[:, dv(h)]
        q_ref[:, qk(h)] = jnp.dot(cb16, wq_ref[h], preferred_element_type=F32).astype(BF16)
        k_ref[:, qk(h)] = (jnp.dot(cb16, wk_ref[h], preferred_element_type=F32)
                           * (M_DQK ** -0.5)).astype(BF16)

    for h in range(M_HEADS):
        conv_branch(h)
        value_projections(h)
        if h > 0:
            key_projections(h - 1)
    ubuf[0:SUBLANES, :] = ubuf[bm:bm + SUBLANES, :]

    gates = bif_ref[...] + jnp.dot(cb_scr[...], wif_ref[...], preferred_element_type=F32)

    lf = _log_sigmoid(gates)
    lf_hi = lf.astype(BF16)
    rem = lf - lf_hi.astype(F32)
    lf_mid = rem.astype(BF16)
    lf_lo = (rem - lf_mid.astype(F32)).astype(BF16)
    tril = (lax.broadcasted_iota(jnp.int32, (bm, bm), 1)
            <= lax.broadcasted_iota(jnp.int32, (bm, bm), 0)).astype(BF16)
    cum = (jnp.dot(tril, lf_hi, preferred_element_type=F32)
           + jnp.dot(tril, lf_mid, preferred_element_type=F32)
           + jnp.dot(tril, lf_lo, preferred_element_type=F32))
    b = pltpu.roll(cum, GATE_PAD - M_HEADS, 1)
    ar = gates - b
    row = lax.broadcasted_iota(jnp.int32, (bm, GATE_PAD), 0)
    cm = ar
    shift = 1
    while shift < bm:
        cm = jnp.maximum(cm, jnp.where(row >= shift, pltpu.roll(cm, shift, 0), -jnp.inf))
        shift *= 2
    lane = lax.broadcasted_iota(jnp.int32, (bm, GATE_PAD), 1)
    gc_ref[...] = jnp.where(lane < M_HEADS, b,
                            jnp.where(lane < 2 * M_HEADS, pltpu.roll(ar, M_HEADS, 1),
                                      pltpu.roll(cm, 2 * M_HEADS, 1)))
    gr_ref[...] = ar.T[0:2 * M_HEADS, :]

    for h in range(M_HEADS):
        gate_branch(h)
    key_projections(M_HEADS - 1)


def _mlstm_pre(x, g, w_in, conv_w, conv_b, w_q, w_k, w_v, w_o, w_if, b_if, skip):
    B, T, D = x.shape
    bm = min(PRE_BM, T)
    nt = T // bm
    full = lambda *shape: pl.BlockSpec(shape, lambda b, t: (0,) * len(shape))
    tok = lambda width: pl.BlockSpec((None, bm, width), lambda b, t: (b, t, 0))
    out_shape = (
        jax.ShapeDtypeStruct((B, T, M_HEADS * M_DQK), BF16),
        jax.ShapeDtypeStruct((B, T, M_HEADS * M_DQK), BF16),
        jax.ShapeDtypeStruct((B, T, M_INNER), BF16),
        jax.ShapeDtypeStruct((B, T, M_INNER), BF16),
        jax.ShapeDtypeStruct((B, T, M_INNER), BF16),
        jax.ShapeDtypeStruct((B, T, M_INNER), BF16),
        jax.ShapeDtypeStruct((B, T, GATE_PAD), F32),
        jax.ShapeDtypeStruct((B, 2 * M_HEADS, T), F32),
    )
    return pl.pallas_call(
        _mlstm_pre_kernel,
        grid=(B, nt),
        in_specs=[
            tok(D), full(1, D), full(D, 2 * M_INNER), full(CONV_K, M_INNER), full(1, M_INNER),
            full(M_HEADS, M_DV, M_DQK), full(M_HEADS, M_DV, M_DQK),
            full(M_HEADS, M_DV, M_DV), full(M_HEADS, M_DV, M_DV),
            full(M_INNER, GATE_PAD), full(1, GATE_PAD), full(1, M_INNER),
        ],
        out_specs=(
            tok(M_HEADS * M_DQK), tok(M_HEADS * M_DQK), tok(M_INNER), tok(M_INNER), tok(M_INNER),
            tok(M_INNER), tok(GATE_PAD),
            pl.BlockSpec((None, 2 * M_HEADS, bm), lambda b, t: (b, 0, t)),
        ),
        out_shape=out_shape,
        scratch_shapes=[pltpu.VMEM((SUBLANES + bm, M_INNER), F32),
                        pltpu.VMEM((bm, M_INNER), BF16)],
        compiler_params=pltpu.CompilerParams(
            dimension_semantics=("parallel", "arbitrary"), vmem_limit_bytes=VMEM_LIMIT),
        name="mlstm_pre",
    )(x, g, w_in, conv_w, conv_b, w_q, w_k, w_v, w_o, w_if, b_if, skip)


def _mlstm_rec_kernel(q_ref, k_ref, v_ref, o_ref, cs_ref, sz_ref, x_ref, gc_ref, gr_ref,
                      og_ref, wout_ref,
                      out_ref,
                      ctx_ref, ctxb_ref, m_ref, s_scr, qc_scr, p_scr, kw_scr, pv_scr, cu_scr, y_scr):
    t = pl.program_id(1)
    L = q_ref.shape[0]

    @pl.when(t == 0)
    def _():
        ctx_ref[...] = jnp.zeros(ctx_ref.shape, F32)
        ctxb_ref[...] = jnp.zeros(ctxb_ref.shape, BF16)
        m_ref[...] = jnp.zeros(m_ref.shape, F32)

    heads = range(M_HEADS)
    qk = lambda h: slice(h * M_DQK, (h + 1) * M_DQK)
    dv = lambda h: slice(h * M_DV, (h + 1) * M_DV)
    ones_tile = jnp.ones((L, LANES), BF16)
    vx = lambda h: jnp.concatenate([v_ref[:, dv(h)], ones_tile], axis=1)

    for h in heads:
        qh = q_ref[:, qk(h)]
        s_scr[h] = lax.dot_general(qh, k_ref[:, qk(h)], NT_DIMS, preferred_element_type=F32)
        qc_scr[h] = jnp.dot(qh, ctxb_ref[h], preferred_element_type=F32)

    gc = gc_ref[...]
    b = gc
    ar = pltpu.roll(gc, GATE_PAD - M_HEADS, 1)
    cm = pltpu.roll(gc, GATE_PAD - 2 * M_HEADS, 1)
    m_prev = m_ref[...]
    g_last = b[L - 1:L, :]
    a_c = g_last + ar
    m_new = jnp.maximum(g_last + m_prev, jnp.max(a_c, axis=0, keepdims=True))
    w_all = jnp.exp(a_c - m_new)
    decay = jnp.exp(g_last + m_prev - m_new)
    mx = jnp.maximum(m_prev, cm)
    inter_all = jnp.exp(m_prev - mx)
    enmt_all = jnp.exp(-(b + mx))
    m_ref[...] = m_new
    gr = gr_ref[...]
    tril = (lax.broadcasted_iota(jnp.int32, (L, L), 1)
            <= lax.broadcasted_iota(jnp.int32, (L, L), 0))

    for h in heads:
        d = jnp.where(tril, jnp.exp(gr[h:h + 1, :] - mx[:, h:h + 1]), 0.0)
        p_scr[h] = (s_scr[h] * d).astype(BF16)
        kw_scr[h] = (k_ref[:, qk(h)].astype(F32) * w_all[:, h:h + 1]).astype(BF16)

    for h in heads:
        pv_scr[h] = jnp.dot(p_scr[h], vx(h), preferred_element_type=F32)
        cu_scr[h] = lax.dot_general(kw_scr[h], vx(h), TN_DIMS, preferred_element_type=F32)

    for h in heads:
        inter = jnp.broadcast_to(inter_all[:, h:h + 1], (L, LANES))
        enmt = jnp.broadcast_to(enmt_all[:, h:h + 1], (L, LANES))
        num = pv_scr[h, :, 0:M_DV] + jnp.tile(inter, (1, M_DV // LANES)) * qc_scr[h, :, 0:M_DV]
        den = pv_scr[h, :, M_DV:M_DV + LANES] + inter * qc_scr[h, :, M_DV:M_DV + LANES]
        r = 1.0 / jnp.maximum(jnp.abs(den), enmt)
        ms = jnp.mean(num * num, axis=1, keepdims=True)
        scale = r * lax.rsqrt(r * r * ms + RMS_EPS)
        hn = (num * jnp.tile(scale, (1, M_DV // LANES)) * og_ref[:, dv(h)]).astype(BF16)
        y_scr[:, dv(h)] = (o_ref[:, dv(h)] * hn + cs_ref[:, dv(h)]) * sz_ref[:, dv(h)]
        part = jnp.dot(y_scr[:, dv(h)], wout_ref[dv(h), :], preferred_element_type=F32)
        if h == 0:
            out_ref[...] = x_ref[...] + part
        else:
            out_ref[...] += part

    for h in heads:
        ctx = decay[:, h:h + 1] * ctx_ref[h] + cu_scr[h]
        ctx_ref[h] = ctx
        ctxb_ref[h] = ctx.astype(BF16)


def _mlstm_rec(q, k, v, o, cs, sz, x, gc, gr, out_g, w_out):
    B, T, D = x.shape
    L = min(REC_L, T)
    nt = T // L
    dvx = M_DV + LANES
    full = lambda *shape: pl.BlockSpec(shape, lambda b, t: (0,) * len(shape))
    tok = lambda width: pl.BlockSpec((None, L, width), lambda b, t: (b, t, 0))
    return pl.pallas_call(
        _mlstm_rec_kernel,
        grid=(B, nt),
        in_specs=[
            tok(M_HEADS * M_DQK), tok(M_HEADS * M_DQK), tok(M_INNER), tok(M_INNER), tok(M_INNER),
            tok(M_INNER), tok(D), tok(GATE_PAD),
            pl.BlockSpec((None, 2 * M_HEADS, L), lambda b, t: (b, 0, t)),
            full(1, M_INNER), full(M_INNER, D),
        ],
        out_specs=tok(D),
        out_shape=jax.ShapeDtypeStruct((B, T, D), F32),
        scratch_shapes=[
            pltpu.VMEM((M_HEADS, M_DQK, dvx), F32),
            pltpu.VMEM((M_HEADS, M_DQK, dvx), BF16),
            pltpu.VMEM((1, LANES), F32),
            pltpu.VMEM((M_HEADS, L, L), F32),
            pltpu.VMEM((M_HEADS, L, dvx), F32),
            pltpu.VMEM((M_HEADS, L, L), BF16),
            pltpu.VMEM((M_HEADS, L, M_DQK), BF16),
            pltpu.VMEM((M_HEADS, L, dvx), F32),
            pltpu.VMEM((M_HEADS, M_DQK, dvx), F32),
            pltpu.VMEM((L, M_INNER), BF16),
        ],
        compiler_params=pltpu.CompilerParams(
            dimension_semantics=("parallel", "arbitrary"), vmem_limit_bytes=VMEM_LIMIT),
        name="mlstm_rec",
    )(q, k, v, o, cs, sz, x, gc, gr, out_g, w_out)


def _bias_kernel(tab_ref, out_ref):
    r = lax.broadcasted_iota(jnp.int32, (REL_SIZE, BIAS_W), 0)
    e = lax.broadcasted_iota(jnp.int32, (REL_SIZE, BIAS_W), 1)
    e = jnp.where(e >= BAND, e - BIAS_W, e)
    idx = jnp.clip(LEFT_CHUNKS * CHUNK - e, -REL_FUTURE, REL_PAST) + REL_FUTURE
    onehot = (r == idx).astype(F32)
    g = jnp.dot(tab_ref[...], onehot, precision=lax.Precision.HIGHEST,
                preferred_element_type=F32) * LOG2_E
    for j in range(A_PAIRS):
        halves = []
        for a in range(2):
            rows = jnp.broadcast_to(g[2 * j + a:2 * j + a + 1, :], (CHUNK, BIAS_W))
            halves.append(pltpu.roll(rows, 0, 1, stride=1, stride_axis=0))
        both = jnp.concatenate(halves, axis=0)
        col = lax.broadcasted_iota(jnp.int32, (2 * CHUNK, BAND2), 1)
        first = jnp.where(col < BAND, both, -jnp.inf)
        second = jnp.where(col >= CHUNK, pltpu.roll(both, CHUNK, 1), -jnp.inf)
        out_ref[j] = jnp.concatenate([first, second], axis=0)


def _attn_bias(rel_bias):
    return pl.pallas_call(
        _bias_kernel,
        out_shape=jax.ShapeDtypeStruct((A_PAIRS, 4 * CHUNK, BAND2), F32),
        compiler_params=pltpu.CompilerParams(vmem_limit_bytes=VMEM_LIMIT),
        name="attn_bias",
    )(rel_bias)


def _attn_pre_kernel(x_ref, g_ref, win_ref, gain_ref,
                     q_ref, k_ref, v_ref, sz_ref,
                     qk_scr):
    bm = x_ref.shape[0]
    xn = (_rms(x_ref[...]) * g_ref[...]).astype(BF16)
    qk_scr[...] = jnp.dot(xn, win_ref[:, 0:2 * A_INNER], preferred_element_type=F32)
    v_ref[...] = jnp.dot(xn, win_ref[:, 2 * A_INNER:3 * A_INNER],
                         preferred_element_type=F32).astype(BF16)
    z = jnp.dot(xn, win_ref[:, 3 * A_INNER:4 * A_INNER], preferred_element_type=F32)
    sz_ref[...] = (z * _sigmoid(z)).astype(BF16)

    first_head = lax.broadcasted_iota(jnp.int32, (bm, 2 * A_HD), 1) < A_HD
    for j in range(2 * A_PAIRS):
        lanes = slice(j * 2 * A_HD, (j + 1) * 2 * A_HD)
        xt = qk_scr[:, lanes]
        sq = xt * xt
        both = jnp.sum(sq, axis=1, keepdims=True)
        head_a = jnp.sum(jnp.where(first_head, sq, 0.0), axis=1, keepdims=True)
        inv_a = lax.rsqrt(head_a * (1.0 / A_HD) + RMS_EPS)
        inv_b = lax.rsqrt((both - head_a) * (1.0 / A_HD) + RMS_EPS)
        out = (xt * jnp.where(first_head, inv_a, inv_b) * gain_ref[:, lanes]).astype(BF16)
        if j < A_PAIRS:
            q_ref[:, lanes] = out
        else:
            k_ref[:, j * 2 * A_HD - A_INNER:(j + 1) * 2 * A_HD - A_INNER] = out


def _attn_pre(x, g, w_in, gain):
    B, T, D = x.shape
    bm = min(APRE_BM, T)
    nt = T // bm
    full = lambda *shape: pl.BlockSpec(shape, lambda b, t: (0,) * len(shape))
    tok = lambda width: pl.BlockSpec((None, bm, width), lambda b, t: (b, t, 0))
    o = jax.ShapeDtypeStruct((B, T, A_INNER), BF16)
    return pl.pallas_call(
        _attn_pre_kernel,
        grid=(B, nt),
        in_specs=[tok(D), full(1, D), full(D, 4 * A_INNER), full(1, 2 * A_INNER)],
        out_specs=(tok(A_INNER),) * 4,
        out_shape=(o, o, o, o),
        scratch_shapes=[pltpu.VMEM((bm, 2 * A_INNER), F32)],
        compiler_params=pltpu.CompilerParams(
            dimension_semantics=("parallel", "parallel"), vmem_limit_bytes=VMEM_LIMIT),
        name="attn_pre",
    )(x, g, w_in, gain)


def _attn_kernel(q_ref, kp_ref, kc_ref, vp_ref, vc_ref, sz_ref, x_ref, bias_ref, wout_ref,
                 out_ref,
                 kcat, vcat, o_s, s_even, s_odd, m_even, m_odd):
    i = pl.program_id(1)
    bm = q_ref.shape[0]
    group = 2 * CHUNK
    n_groups = bm // group
    kcat[0:bm, :] = kp_ref[...]
    kcat[bm:2 * bm, :] = kc_ref[...]
    vcat[0:bm, :] = vp_ref[...]
    vcat[bm:2 * bm, :] = vc_ref[...]
    first_head = lax.broadcasted_iota(jnp.int32, (CHUNK, 2 * A_HD), 1) < A_HD
    key_off = lax.broadcasted_iota(jnp.int32, (1, BAND2), 1)
    ones_tile = jnp.ones((BAND2, LANES), BF16)

    lanes = lambda j: slice(j * 2 * A_HD, (j + 1) * 2 * A_HD)
    rows = lambda g: slice(g * group, (g + 1) * group)
    band = lambda g: slice(g * group + bm - LEFT_CHUNKS * CHUNK,
                           g * group + bm - LEFT_CHUNKS * CHUNK + BAND2)

    def scores(masked, g, j, s_ref, m_ref):
        q4 = q_ref[rows(g), lanes(j)].astype(F32)
        parts = []
        for c in range(2):
            qc = q4[c * CHUNK:(c + 1) * CHUNK, :]
            parts += [jnp.where(first_head, qc, 0.0), jnp.where(first_head, 0.0, qc)]
        wt = jnp.concatenate(parts, axis=0).astype(BF16)
        s = lax.dot_general(wt, kcat[band(g), lanes(j)], NT_DIMS,
                            preferred_element_type=F32) + bias_ref[j]
        if masked:
            valid = (g * group - LEFT_CHUNKS * CHUNK + key_off) >= 0
            s = jnp.where(valid, s, -jnp.inf)
        s_ref[j] = s
        m_ref[j] = jnp.broadcast_to(jnp.max(s, axis=1, keepdims=True), (2 * group, LANES))

    def finish(g, j, s_ref, m_ref):
        m = m_ref[j]
        e = jnp.exp2(s_ref[j] - jnp.concatenate([m] * (BAND2 // LANES), axis=1))
        v2 = jnp.concatenate([vcat[band(g), lanes(j)], ones_tile], axis=1)
        r = jnp.dot(e.astype(BF16), v2, preferred_element_type=F32)
        r = r[:, 0:2 * A_HD] * (1.0 / r[:, 2 * A_HD:4 * A_HD])
        o4 = jnp.concatenate(
            [jnp.where(first_head, r[2 * c * CHUNK:(2 * c + 1) * CHUNK, :],
                       r[(2 * c + 1) * CHUNK:(2 * c + 2) * CHUNK, :]) for c in range(2)], axis=0)
        o_s[rows(g), lanes(j)] = (o4 * sz_ref[rows(g), lanes(j)].astype(F32)).astype(BF16)

    even, odd = (s_even, m_even), (s_odd, m_odd)

    def run(masked):
        for j in range(A_PAIRS):
            scores(masked, 0, j, *even)
        for g in range(n_groups):
            cur, nxt = (even, odd) if g % 2 == 0 else (odd, even)
            for j in range(A_PAIRS):
                finish(g, j, *cur)
                if g + 1 < n_groups:
                    scores(masked, g + 1, j, *nxt)

    @pl.when(i == 0)
    def _():
        run(True)

    @pl.when(i > 0)
    def _():
        run(False)

    out_ref[...] = x_ref[...] + jnp.dot(o_s[...], wout_ref[...], preferred_element_type=F32)


def _attn(q, k, v, sz, x, bias, w_out):
    B, T, D = x.shape
    bm = ATT_BM
    nt = T // bm
    once = lambda *shape: pl.BlockSpec(shape, lambda b, t: (0,) * len(shape),
                                       pipeline_mode=pl.Buffered(1))
    cur = lambda width: pl.BlockSpec((None, bm, width), lambda b, t: (b, t, 0))
    prev = lambda width: pl.BlockSpec((None, bm, width), lambda b, t: (b, jnp.maximum(t - 1, 0), 0))
    return pl.pallas_call(
        _attn_kernel,
        grid=(B, nt),
        in_specs=[cur(A_INNER), prev(A_INNER), cur(A_INNER), prev(A_INNER), cur(A_INNER),
                  cur(A_INNER), cur(D), once(A_PAIRS, 4 * CHUNK, BAND2), once(A_INNER, D)],
        out_specs=cur(D),
        out_shape=jax.ShapeDtypeStruct((B, T, D), F32),
        scratch_shapes=[
            pltpu.VMEM((2 * bm, A_INNER), BF16),
            pltpu.VMEM((2 * bm, A_INNER), BF16),
            pltpu.VMEM((bm, A_INNER), BF16),
            pltpu.VMEM((A_PAIRS, 4 * CHUNK, BAND2), F32),
            pltpu.VMEM((A_PAIRS, 4 * CHUNK, BAND2), F32),
            pltpu.VMEM((A_PAIRS, 4 * CHUNK, LANES), F32),
            pltpu.VMEM((A_PAIRS, 4 * CHUNK, LANES), F32),
        ],
        compiler_params=pltpu.CompilerParams(
            dimension_semantics=("parallel", "parallel"), vmem_limit_bytes=VMEM_LIMIT),
        name="attn",
    )(q, k, k, v, v, sz, x, bias, w_out)


def kernel(x, norm_g, a_w_in, a_conv_w, a_conv_b, a_w_q, a_w_k, a_w_v, a_w_o, a_w_if, a_b_if,
           a_out_g, a_skip, a_w_out, b_w_in, b_q_g, b_k_g, b_rel_bias, b_w_out):
    assert x.shape[1] % ATT_BM == 0 and x.shape[2] == D_MODEL
    bf = lambda w: w.astype(BF16)
    pad_gate = lambda w: jnp.pad(w, ((0, 0), (0, GATE_PAD - 2 * M_HEADS)))

    q, k, v, o, cs, sz, gc, gr = _mlstm_pre(
        x, norm_g[0][None, :], bf(a_w_in[0]), a_conv_w[0], a_conv_b[0][None, :],
        bf(a_w_q[0]), bf(a_w_k[0]), bf(a_w_v[0]), bf(a_w_o[0]),
        bf(pad_gate(a_w_if[0])), pad_gate(a_b_if[0][None, :]), a_skip[0][None, :])
    x1 = _mlstm_rec(q, k, v, o, cs, sz, x, gc, gr, a_out_g[0].reshape(1, M_INNER), bf(a_w_out[0]))

    bias = _attn_bias(b_rel_bias[0])
    gain = jnp.concatenate([jnp.tile(b_q_g[0], A_HEADS) * (A_HD ** -0.5 * LOG2_E),
                            jnp.tile(b_k_g[0], A_HEADS)])[None, :]
    q, k, v, sz = _attn_pre(x1, norm_g[1][None, :], bf(b_w_in[0]), gain)
    return _attn(q, k, v, sz, x1, bias, bf(b_w_out[0]))
```

```python
import functools

import jax
import jax.numpy as jnp
from jax import lax
from jax.experimental import pallas as pl
from jax.experimental.pallas import tpu as pltpu

F32 = jnp.float32
BF16 = jnp.bfloat16

RMS_EPS = 1e-6
CHUNK = 64
LOG2_E = 1.4426950408889634

D_MODEL = 1024
M_INNER = 2 * D_MODEL
M_HEADS = 4
M_DV = M_INNER // M_HEADS
M_DQK = M_DV // 2
CONV_K = 4

A_INNER = D_MODEL
A_HEADS = 16
A_HD = A_INNER // A_HEADS
A_PAIRS = A_HEADS // 2
LEFT_CHUNKS = 8
BAND = (LEFT_CHUNKS + 1) * CHUNK
REL_PAST = 256
REL_FUTURE = CHUNK - 1
REL_SIZE = REL_PAST + REL_FUTURE + 1

LANES = 128
SUBLANES = 8
GATE_PAD = LANES
VMEM_LIMIT = 56 * 1024 * 1024
FUSED_VMEM_LIMIT = 60 * 1024 * 1024
PRE_BM = 256
REC_L = 256
APRE_BM = 256
ATT_BM = LEFT_CHUNKS * CHUNK
BAND2 = BAND + CHUNK
BIAS_W = BAND2

NT_DIMS = (((1,), (1,)), ((), ()))
TN_DIMS = (((0,), (0,)), ((), ()))


def _sigmoid(x):
    return 1.0 / (1.0 + jnp.exp(-x))


def _log_sigmoid(x):
    return jnp.minimum(x, 0.0) - jnp.log1p(jnp.exp(-jnp.abs(x)))


def _rms(x, eps=RMS_EPS):
    return x * lax.rsqrt(jnp.mean(x * x, axis=-1, keepdims=True) + eps)


def _mlstm_pre_kernel(x_ref, g_ref, win_ref, cw_ref, cb_ref, wq_ref, wk_ref, wv_ref, wo_ref,
                      wif_ref, bif_ref, skip_ref,
                      q_ref, k_ref, v_ref, o_ref, cs_ref, sz_ref, gc_ref, gr_ref,
                      ubuf, cb_scr):
    t = pl.program_id(1)
    bm = x_ref.shape[0]
    dv = lambda h: slice(h * M_DV, (h + 1) * M_DV)
    qk = lambda h: slice(h * M_DQK, (h + 1) * M_DQK)

    @pl.when(t == 0)
    def _():
        ubuf[0:SUBLANES, :] = jnp.zeros((SUBLANES, M_INNER), F32)

    xn = (_rms(x_ref[...]) * g_ref[...]).astype(BF16)

    for h in range(M_HEADS):
        ubuf[SUBLANES:SUBLANES + bm, dv(h)] = jnp.dot(xn, win_ref[:, dv(h)],
                                                      preferred_element_type=F32)

    def gate_branch(h):
        z = jnp.dot(xn, win_ref[:, M_INNER + h * M_DV:M_INNER + (h + 1) * M_DV],
                    preferred_element_type=F32)
        sz_ref[:, dv(h)] = (z * _sigmoid(z)).astype(BF16)

    def conv_branch(h):
        u = ubuf[SUBLANES:SUBLANES + bm, dv(h)]
        conv = cb_ref[:, dv(h)] + cw_ref[CONV_K - 1:CONV_K, dv(h)] * u
        for j in range(CONV_K - 1):
            shifted = ubuf[pl.ds(SUBLANES - (CONV_K - 1) + j, bm), dv(h)]
            conv = conv + cw_ref[j:j + 1, dv(h)] * shifted
        c = conv * _sigmoid(conv)
        cb_scr[:, dv(h)] = c.astype(BF16)
        cs_ref[:, dv(h)] = (skip_ref[:, dv(h)] * c).astype(BF16)

    def value_projections(h):
        ub16 = ubuf[SUBLANES:SUBLANES + bm, dv(h)].astype(BF16)
        v_ref[:, dv(h)] = jnp.dot(ub16, wv_ref[h], preferred_element_type=F32).astype(BF16)
        o_ref[:, dv(h)] = _sigmoid(jnp.dot(ub16, wo_ref[h], preferred_element_type=F32)).astype(BF16)

    def key_projections(h):
        cb16 = cb_scr[:, dv(h)]
        q_ref[:, qk(h)] = jnp.dot(cb16, wq_ref[h], preferred_element_type=F32).astype(BF16)
        k_ref[:, qk(h)] = (jnp.dot(cb16, wk_ref[h], preferred_element_type=F32)
                           * (M_DQK ** -0.5)).astype(BF16)

    for h in range(M_HEADS):
        conv_branch(h)
        value_projections(h)
        if h > 0:
            key_projections(h - 1)
    ubuf[0:SUBLANES, :] = ubuf[bm:bm + SUBLANES, :]

    gates = bif_ref[...] + jnp.dot(cb_scr[...], wif_ref[...], preferred_element_type=F32)

    lf = _log_sigmoid(gates)
    lf_hi = lf.astype(BF16)
    rem = lf - lf_hi.astype(F32)
    lf_mid = rem.astype(BF16)
    lf_lo = (rem - lf_mid.astype(F32)).astype(BF16)
    tril = (lax.broadcasted_iota(jnp.int32, (bm, bm), 1)
            <= lax.broadcasted_iota(jnp.int32, (bm, bm), 0)).astype(BF16)
    cum = (jnp.dot(tril, lf_hi, preferred_element_type=F32)
           + jnp.dot(tril, lf_mid, preferred_element_type=F32)
           + jnp.dot(tril, lf_lo, preferred_element_type=F32))
    b = pltpu.roll(cum, GATE_PAD - M_HEADS, 1)
    ar = gates - b
    row = lax.broadcasted_iota(jnp.int32, (bm, GATE_PAD), 0)
    cm = ar
    shift = 1
    while shift < bm:
        cm = jnp.maximum(cm, jnp.where(row >= shift, pltpu.roll(cm, shift, 0), -jnp.inf))
        shift *= 2
    lane = lax.broadcasted_iota(jnp.int32, (bm, GATE_PAD), 1)
    gc_ref[...] = jnp.where(lane < M_HEADS, b,
                            jnp.where(lane < 2 * M_HEADS, pltpu.roll(ar, M_HEADS, 1),
                                      pltpu.roll(cm, 2 * M_HEADS, 1)))
    gr_ref[...] = ar.T[0:2 * M_HEADS, :]

    for h in range(M_HEADS):
        gate_branch(h)
    key_projections(M_HEADS - 1)


def _mlstm_pre(x, g, w_in, conv_w, conv_b, w_q, w_k, w_v, w_o, w_if, b_if, skip):
    B, T, D = x.shape
    bm = min(PRE_BM, T)
    nt = T // bm
    full = lambda *shape: pl.BlockSpec(shape, lambda b, t: (0,) * len(shape))
    tok = lambda width: pl.BlockSpec((None, bm, width), lambda b, t: (b, t, 0))
    out_shape = (
        jax.ShapeDtypeStruct((B, T, M_HEADS * M_DQK), BF16),
        jax.ShapeDtypeStruct((B, T, M_HEADS * M_DQK), BF16),
        jax.ShapeDtypeStruct((B, T, M_INNER), BF16),
        jax.ShapeDtypeStruct((B, T, M_INNER), BF16),
        jax.ShapeDtypeStruct((B, T, M_INNER), BF16),
        jax.ShapeDtypeStruct((B, T, M_INNER), BF16),
        jax.ShapeDtypeStruct((B, T, GATE_PAD), F32),
        jax.ShapeDtypeStruct((B, 2 * M_HEADS, T), F32),
    )
    return pl.pallas_call(
        _mlstm_pre_kernel,
        grid=(B, nt),
        in_specs=[
            tok(D), full(1, D), full(D, 2 * M_INNER), full(CONV_K, M_INNER), full(1, M_INNER),
            full(M_HEADS, M_DV, M_DQK), full(M_HEADS, M_DV, M_DQK),
            full(M_HEADS, M_DV, M_DV), full(M_HEADS, M_DV, M_DV),
            full(M_INNER, GATE_PAD), full(1, GATE_PAD), full(1, M_INNER),
        ],
        out_specs=(
            tok(M_HEADS * M_DQK), tok(M_HEADS * M_DQK), tok(M_INNER), tok(M_INNER), tok(M_INNER),
            tok(M_INNER), tok(GATE_PAD),
            pl.BlockSpec((None, 2 * M_HEADS, bm), lambda b, t: (b, 0, t)),
        ),
        out_shape=out_shape,
        scratch_shapes=[pltpu.VMEM((SUBLANES + bm, M_INNER), F32),
                        pltpu.VMEM((bm, M_INNER), BF16)],
        compiler_params=pltpu.CompilerParams(
            dimension_semantics=("parallel", "arbitrary"), vmem_limit_bytes=VMEM_LIMIT),
        name="mlstm_pre",
    )(x, g, w_in, conv_w, conv_b, w_q, w_k, w_v, w_o, w_if, b_if, skip)


def _mlstm_rec_kernel(q_ref, k_ref, v_ref, o_ref, cs_ref, sz_ref, x_ref, gc_ref, gr_ref,
                      og_ref, wout_ref,
                      out_ref,
                      ctx_ref, ctxb_ref, m_ref, s_scr, qc_scr, p_scr, kw_scr, pv_scr, cu_scr, y_scr):
    t = pl.program_id(1)
    L = q_ref.shape[0]

    @pl.when(t == 0)
    def _():
        ctx_ref[...] = jnp.zeros(ctx_ref.shape, F32)
        ctxb_ref[...] = jnp.zeros(ctxb_ref.shape, BF16)
        m_ref[...] = jnp.zeros(m_ref.shape, F32)

    heads = range(M_HEADS)
    qk = lambda h: slice(h * M_DQK, (h + 1) * M_DQK)
    dv = lambda h: slice(h * M_DV, (h + 1) * M_DV)
    ones_tile = jnp.ones((L, LANES), BF16)
    vx = lambda h: jnp.concatenate([v_ref[:, dv(h)], ones_tile], axis=1)

    for h in heads:
        qh = q_ref[:, qk(h)]
        s_scr[h] = lax.dot_general(qh, k_ref[:, qk(h)], NT_DIMS, preferred_element_type=F32)
        qc_scr[h] = jnp.dot(qh, ctxb_ref[h], preferred_element_type=F32)

    gc = gc_ref[...]
    b = gc
    ar = pltpu.roll(gc, GATE_PAD - M_HEADS, 1)
    cm = pltpu.roll(gc, GATE_PAD - 2 * M_HEADS, 1)
    m_prev = m_ref[...]
    g_last = b[L - 1:L, :]
    a_c = g_last + ar
    m_new = jnp.maximum(g_last + m_prev, jnp.max(a_c, axis=0, keepdims=True))
    w_all = jnp.exp(a_c - m_new)
    decay = jnp.exp(g_last + m_prev - m_new)
    mx = jnp.maximum(m_prev, cm)
    inter_all = jnp.exp(m_prev - mx)
    enmt_all = jnp.exp(-(b + mx))
    m_ref[...] = m_new
    gr = gr_ref[...]
    tril = (lax.broadcasted_iota(jnp.int32, (L, L), 1)
            <= lax.broadcasted_iota(jnp.int32, (L, L), 0))

    for h in heads:
        d = jnp.where(tril, jnp.exp(gr[h:h + 1, :] - mx[:, h:h + 1]), 0.0)
        p_scr[h] = (s_scr[h] * d).astype(BF16)
        kw_scr[h] = (k_ref[:, qk(h)].astype(F32) * w_all[:, h:h + 1]).astype(BF16)

    for h in heads:
        pv_scr[h] = jnp.dot(p_scr[h], vx(h), preferred_element_type=F32)
        cu_scr[h] = lax.dot_general(kw_scr[h], vx(h), TN_DIMS, preferred_element_type=F32)

    for h in heads:
        inter = jnp.broadcast_to(inter_all[:, h:h + 1], (L, LANES))
        enmt = jnp.broadcast_to(enmt_all[:, h:h + 1], (L, LANES))
        num = pv_scr[h, :, 0:M_DV] + jnp.tile(inter, (1, M_DV // LANES)) * qc_scr[h, :, 0:M_DV]
        den = pv_scr[h, :, M_DV:M_DV + LANES] + inter * qc_scr[h, :, M_DV:M_DV + LANES]
        r = 1.0 / jnp.maximum(jnp.abs(den), enmt)
        ms = jnp.mean(num * num, axis=1, keepdims=True)
        scale = r * lax.rsqrt(r * r * ms + RMS_EPS)
        hn = (num * jnp.tile(scale, (1, M_DV // LANES)) * og_ref[:, dv(h)]).astype(BF16)
        y_scr[:, dv(h)] = (o_ref[:, dv(h)] * hn + cs_ref[:, dv(h)]) * sz_ref[:, dv(h)]
        part = jnp.dot(y_scr[:, dv(h)], wout_ref[dv(h), :], preferred_element_type=F32)
        if h == 0:
            out_ref[...] = x_ref[...] + part
        else:
            out_ref[...] += part

    for h in heads:
        ctx = decay[:, h:h + 1] * ctx_ref[h] + cu_scr[h]
        ctx_ref[h] = ctx
        ctxb_ref[h] = ctx.astype(BF16)


def _mlstm_rec(q, k, v, o, cs, sz, x, gc, gr, out_g, w_out):
    B, T, D = x.shape
    L = min(REC_L, T)
    nt = T // L
    dvx = M_DV + LANES
    full = lambda *shape: pl.BlockSpec(shape, lambda b, t: (0,) * len(shape))
    tok = lambda width: pl.BlockSpec((None, L, width), lambda b, t: (b, t, 0))
    return pl.pallas_call(
        _mlstm_rec_kernel,
        grid=(B, nt),
        in_specs=[
            tok(M_HEADS * M_DQK), tok(M_HEADS * M_DQK), tok(M_INNER), tok(M_INNER), tok(M_INNER),
            tok(M_INNER), tok(D), tok(GATE_PAD),
            pl.BlockSpec((None, 2 * M_HEADS, L), lambda b, t: (b, 0, t)),
            full(1, M_INNER), full(M_INNER, D),
        ],
        out_specs=tok(D),
        out_shape=jax.ShapeDtypeStruct((B, T, D), F32),
        scratch_shapes=[
            pltpu.VMEM((M_HEADS, M_DQK, dvx), F32),
            pltpu.VMEM((M_HEADS, M_DQK, dvx), BF16),
            pltpu.VMEM((1, LANES), F32),
            pltpu.VMEM((M_HEADS, L, L), F32),
            pltpu.VMEM((M_HEADS, L, dvx), F32),
            pltpu.VMEM((M_HEADS, L, L), BF16),
            pltpu.VMEM((M_HEADS, L, M_DQK), BF16),
            pltpu.VMEM((M_HEADS, L, dvx), F32),
            pltpu.VMEM((M_HEADS, M_DQK, dvx), F32),
            pltpu.VMEM((L, M_INNER), BF16),
        ],
        compiler_params=pltpu.CompilerParams(
            dimension_semantics=("parallel", "arbitrary"), vmem_limit_bytes=VMEM_LIMIT),
        name="mlstm_rec",
    )(q, k, v, o, cs, sz, x, gc, gr, out_g, w_out)


def _mlstm_layer_kernel(xp_ref, xc_ref, g_ref, win_ref, cw_ref, cb_ref, wq_ref, wk_ref, wv_ref, wo_ref,
                        wif_ref, bif_ref, skip_ref, og_ref, wout_ref,
                        out_ref,
                        ubuf, cb_scr, hq, hk, hv, ho, hcs, hsz, hgc, hgr,
                        ctx_ref, ctxb_ref, m_ref, s_scr, qc_scr, p_scr, kw_scr, pv_scr, cu_scr, y_scr,
                        *, blocks_per_row, n_blocks):
    step = pl.program_id(0)
    L = xp_ref.shape[0]
    wr = step % 2
    rd = 1 - wr
    prod_t = jnp.minimum(step, n_blocks - 1) % blocks_per_row
    cons_t = jnp.maximum(step - 1, 0) % blocks_per_row
    heads = range(M_HEADS)
    dv = lambda h: slice(h * M_DV, (h + 1) * M_DV)
    qk = lambda h: slice(h * M_DQK, (h + 1) * M_DQK)

    @pl.when(step == 0)
    def _():
        for ref in (hq, hk, hv, ho, hcs, hsz, hgc, hgr):
            ref[1] = jnp.zeros(ref.shape[1:], ref.dtype)

    @pl.when(prod_t == 0)
    def _():
        ubuf[0:SUBLANES, :] = jnp.zeros((SUBLANES, M_INNER), F32)

    @pl.when(cons_t == 0)
    def _():
        ctx_ref[...] = jnp.zeros(ctx_ref.shape, F32)
        ctxb_ref[...] = jnp.zeros(ctxb_ref.shape, BF16)
        m_ref[...] = jnp.zeros(m_ref.shape, F32)

    for h in heads:
        qh = hq[rd, :, qk(h)]
        s_scr[h] = lax.dot_general(qh, hk[rd, :, qk(h)], NT_DIMS, preferred_element_type=F32)
        qc_scr[h] = jnp.dot(qh, ctxb_ref[h], preferred_element_type=F32)

    xn = (_rms(xp_ref[...]) * g_ref[...]).astype(BF16)

    gc = hgc[rd]
    b = gc
    ar = pltpu.roll(gc, GATE_PAD - M_HEADS, 1)
    cm = pltpu.roll(gc, GATE_PAD - 2 * M_HEADS, 1)
    m_prev = m_ref[...]
    g_last = b[L - 1:L, :]
    a_c = g_last + ar
    m_new = jnp.maximum(g_last + m_prev, jnp.max(a_c, axis=0, keepdims=True))
    w_all = jnp.exp(a_c - m_new)
    decay = jnp.exp(g_last + m_prev - m_new)
    mx = jnp.maximum(m_prev, cm)
    inter_all = jnp.exp(m_prev - mx)
    enmt_all = jnp.exp(-(b + mx))
    m_ref[...] = m_new
    gr = hgr[rd]
    tril = (lax.broadcasted_iota(jnp.int32, (L, L), 1)
            <= lax.broadcasted_iota(jnp.int32, (L, L), 0))

    for h in heads:
        d = jnp.where(tril, jnp.exp(gr[h:h + 1, :] - mx[:, h:h + 1]), 0.0)
        p_scr[h] = (s_scr[h] * d).astype(BF16)
        kw_scr[h] = (hk[rd, :, qk(h)].astype(F32) * w_all[:, h:h + 1]).astype(BF16)

    for h in heads:
        ubuf[SUBLANES:SUBLANES + L, dv(h)] = jnp.dot(xn, win_ref[:, dv(h)],
                                                     preferred_element_type=F32)

    ones_tile = jnp.ones((L, LANES), BF16)
    vx = lambda h: jnp.concatenate([hv[rd, :, dv(h)], ones_tile], axis=1)

    def numerators_and_increments():
        for h in heads:
            pv_scr[h] = jnp.dot(p_scr[h], vx(h), preferred_element_type=F32)
            cu_scr[h] = lax.dot_general(kw_scr[h], vx(h), TN_DIMS, preferred_element_type=F32)

    def conv_branch(h):
        u = ubuf[SUBLANES:SUBLANES + L, dv(h)]
        conv = cb_ref[:, dv(h)] + cw_ref[CONV_K - 1:CONV_K, dv(h)] * u
        for j in range(CONV_K - 1):
            shifted = ubuf[pl.ds(SUBLANES - (CONV_K - 1) + j, L), dv(h)]
            conv = conv + cw_ref[j:j + 1, dv(h)] * shifted
        c = conv * _sigmoid(conv)
        cb_scr[:, dv(h)] = c.astype(BF16)
        hcs[wr, :, dv(h)] = (skip_ref[:, dv(h)] * c).astype(BF16)

    def value_projections(h):
        ub16 = ubuf[SUBLANES:SUBLANES + L, dv(h)].astype(BF16)
        hv[wr, :, dv(h)] = jnp.dot(ub16, wv_ref[h], preferred_element_type=F32).astype(BF16)
        ho[wr, :, dv(h)] = _sigmoid(jnp.dot(ub16, wo_ref[h], preferred_element_type=F32)).astype(BF16)

    def key_projections(h):
        cb16 = cb_scr[:, dv(h)]
        hq[wr, :, qk(h)] = jnp.dot(cb16, wq_ref[h], preferred_element_type=F32).astype(BF16)
        hk[wr, :, qk(h)] = (jnp.dot(cb16, wk_ref[h], preferred_element_type=F32)
                            * (M_DQK ** -0.5)).astype(BF16)

    def gate_branch(h):
        z = jnp.dot(xn, win_ref[:, M_INNER + h * M_DV:M_INNER + (h + 1) * M_DV],
                    preferred_element_type=F32)
        hsz[wr, :, dv(h)] = (z * _sigmoid(z)).astype(BF16)

    def normalise_and_project(h):
        inter = jnp.broadcast_to(inter_all[:, h:h + 1], (L, LANES))
        enmt = jnp.broadcast_to(enmt_all[:, h:h + 1], (L, LANES))
        num = pv_scr[h, :, 0:M_DV] + jnp.tile(inter, (1, M_DV // LANES)) * qc_scr[h, :, 0:M_DV]
        den = pv_scr[h, :, M_DV:M_DV + LANES] + inter * qc_scr[h, :, M_DV:M_DV + LANES]
        r = 1.0 / jnp.maximum(jnp.abs(den), enmt)
        ms = jnp.mean(num * num, axis=1, keepdims=True)
        scale = r * lax.rsqrt(r * r * ms + RMS_EPS)
        hn = (num * jnp.tile(scale, (1, M_DV // LANES)) * og_ref[:, dv(h)]).astype(BF16)
        y_scr[:, dv(h)] = (ho[rd, :, dv(h)] * hn + hcs[rd, :, dv(h)]) * hsz[rd, :, dv(h)]
        part = jnp.dot(y_scr[:, dv(h)], wout_ref[dv(h), :], preferred_element_type=F32)
        if h == 0:
            out_ref[...] = xc_ref[...] + part
        else:
            out_ref[...] += part

    for h in heads:
        conv_branch(h)
    ubuf[0:SUBLANES, :] = ubuf[L:L + SUBLANES, :]
    numerators_and_increments()

    def gate_algebra():
        gates = bif_ref[...] + jnp.dot(cb_scr[...], wif_ref[...], preferred_element_type=F32)
        lf = _log_sigmoid(gates)
        lf_hi = lf.astype(BF16)
        rem = lf - lf_hi.astype(F32)
        lf_mid = rem.astype(BF16)
        lf_lo = (rem - lf_mid.astype(F32)).astype(BF16)
        trilb = tril.astype(BF16)
        cum = (jnp.dot(trilb, lf_hi, preferred_element_type=F32)
               + jnp.dot(trilb, lf_mid, preferred_element_type=F32)
               + jnp.dot(trilb, lf_lo, preferred_element_type=F32))
        nb = pltpu.roll(cum, GATE_PAD - M_HEADS, 1)
        nar = gates - nb
        row = lax.broadcasted_iota(jnp.int32, (L, GATE_PAD), 0)
        ncm = nar
        shift = 1
        while shift < L:
            ncm = jnp.maximum(ncm, jnp.where(row >= shift, pltpu.roll(ncm, shift, 0), -jnp.inf))
            shift *= 2
        lane = lax.broadcasted_iota(jnp.int32, (L, GATE_PAD), 1)
        hgc[wr] = jnp.where(lane < M_HEADS, nb,
                            jnp.where(lane < 2 * M_HEADS, pltpu.roll(nar, M_HEADS, 1),
                                      pltpu.roll(ncm, 2 * M_HEADS, 1)))
        hgr[wr] = nar.T[0:2 * M_HEADS, :]

    for h in heads:
        if h == M_HEADS // 2:
            gate_algebra()
        gate_branch(h)
        value_projections(h)
        key_projections(h)
        normalise_and_project(h)

    for h in heads:
        ctx = decay[:, h:h + 1] * ctx_ref[h] + cu_scr[h]
        ctx_ref[h] = ctx
        ctxb_ref[h] = ctx.astype(BF16)


def _mlstm_layer(x, g, w_in, conv_w, conv_b, w_q, w_k, w_v, w_o, w_if, b_if, skip, out_g, w_out):
    B, T, D = x.shape
    L = REC_L
    nt = T // L
    n_blocks = B * nt
    dvx = M_DV + LANES
    once = lambda *shape: pl.BlockSpec(shape, lambda s: (0,) * len(shape), pipeline_mode=pl.Buffered(1))

    def prod_map(s):
        blk = jnp.minimum(s, n_blocks - 1)
        return (blk // nt, blk % nt, 0)

    def cons_map(s):
        blk = jnp.maximum(s - 1, 0)
        return (blk // nt, blk % nt, 0)

    hand = lambda width, dtype: pltpu.VMEM((2, L, width), dtype)
    return pl.pallas_call(
        functools.partial(_mlstm_layer_kernel, blocks_per_row=nt, n_blocks=n_blocks),
        grid=(n_blocks + 1,),
        in_specs=[
            pl.BlockSpec((None, L, D), prod_map), pl.BlockSpec((None, L, D), cons_map),
            once(1, D), once(D, 2 * M_INNER), once(CONV_K, M_INNER), once(1, M_INNER),
            once(M_HEADS, M_DV, M_DQK), once(M_HEADS, M_DV, M_DQK),
            once(M_HEADS, M_DV, M_DV), once(M_HEADS, M_DV, M_DV),
            once(M_INNER, GATE_PAD), once(1, GATE_PAD), once(1, M_INNER),
            once(1, M_INNER), once(M_INNER, D),
        ],
        out_specs=pl.BlockSpec((None, L, D), cons_map),
        out_shape=jax.ShapeDtypeStruct((B, T, D), F32),
        scratch_shapes=[
            pltpu.VMEM((SUBLANES + L, M_INNER), F32),
            pltpu.VMEM((L, M_INNER), BF16),
            hand(M_HEADS * M_DQK, BF16), hand(M_HEADS * M_DQK, BF16),
            hand(M_INNER, BF16), hand(M_INNER, BF16),
            hand(M_INNER, BF16), hand(M_INNER, BF16),
            hand(GATE_PAD, F32), pltpu.VMEM((2, 2 * M_HEADS, L), F32),
            pltpu.VMEM((M_HEADS, M_DQK, dvx), F32),
            pltpu.VMEM((M_HEADS, M_DQK, dvx), BF16),
            pltpu.VMEM((1, LANES), F32),
            pltpu.VMEM((M_HEADS, L, L), F32),
            pltpu.VMEM((M_HEADS, L, dvx), F32),
            pltpu.VMEM((M_HEADS, L, L), BF16),
            pltpu.VMEM((M_HEADS, L, M_DQK), BF16),
            pltpu.VMEM((M_HEADS, L, dvx), F32),
            pltpu.VMEM((M_HEADS, M_DQK, dvx), F32),
            pltpu.VMEM((L, M_INNER), BF16),
        ],
        compiler_params=pltpu.CompilerParams(
            dimension_semantics=("arbitrary",), vmem_limit_bytes=FUSED_VMEM_LIMIT),
        name="mlstm_layer",
    )(x, x, g, w_in, conv_w, conv_b, w_q, w_k, w_v, w_o, w_if, b_if, skip, out_g, w_out)


def _bias_kernel(tab_ref, out_ref):
    r = lax.broadcasted_iota(jnp.int32, (REL_SIZE, BIAS_W), 0)
    e = lax.broadcasted_iota(jnp.int32, (REL_SIZE, BIAS_W), 1)
    e = jnp.where(e >= BAND, e - BIAS_W, e)
    idx = jnp.clip(LEFT_CHUNKS * CHUNK - e, -REL_FUTURE, REL_PAST) + REL_FUTURE
    onehot = (r == idx).astype(F32)
    g = jnp.dot(tab_ref[...], onehot, precision=lax.Precision.HIGHEST,
                preferred_element_type=F32) * LOG2_E
    for j in range(A_PAIRS):
        halves = []
        for a in range(2):
            rows = jnp.broadcast_to(g[2 * j + a:2 * j + a + 1, :], (CHUNK, BIAS_W))
            halves.append(pltpu.roll(rows, 0, 1, stride=1, stride_axis=0))
        both = jnp.concatenate(halves, axis=0)
        col = lax.broadcasted_iota(jnp.int32, (2 * CHUNK, BAND2), 1)
        first = jnp.where(col < BAND, both, -jnp.inf)
        second = jnp.where(col >= CHUNK, pltpu.roll(both, CHUNK, 1), -jnp.inf)
        out_ref[j] = jnp.concatenate([first, second], axis=0)


def _attn_bias(rel_bias):
    return pl.pallas_call(
        _bias_kernel,
        out_shape=jax.ShapeDtypeStruct((A_PAIRS, 4 * CHUNK, BAND2), F32),
        compiler_params=pltpu.CompilerParams(vmem_limit_bytes=VMEM_LIMIT),
        name="attn_bias",
    )(rel_bias)


def _attn_pre_kernel(x_ref, g_ref, win_ref, gain_ref,
                     q_ref, k_ref, v_ref, sz_ref,
                     qk_scr):
    bm = x_ref.shape[0]
    xn = (_rms(x_ref[...]) * g_ref[...]).astype(BF16)
    qk_scr[...] = jnp.dot(xn, win_ref[:, 0:2 * A_INNER], preferred_element_type=F32)
    v_ref[...] = jnp.dot(xn, win_ref[:, 2 * A_INNER:3 * A_INNER],
                         preferred_element_type=F32).astype(BF16)
    z = jnp.dot(xn, win_ref[:, 3 * A_INNER:4 * A_INNER], preferred_element_type=F32)
    sz_ref[...] = (z * _sigmoid(z)).astype(BF16)

    first_head = lax.broadcasted_iota(jnp.int32, (bm, 2 * A_HD), 1) < A_HD
    for j in range(2 * A_PAIRS):
        lanes = slice(j * 2 * A_HD, (j + 1) * 2 * A_HD)
        xt = qk_scr[:, lanes]
        sq = xt * xt
        both = jnp.sum(sq, axis=1, keepdims=True)
        head_a = jnp.sum(jnp.where(first_head, sq, 0.0), axis=1, keepdims=True)
        inv_a = lax.rsqrt(head_a * (1.0 / A_HD) + RMS_EPS)
        inv_b = lax.rsqrt((both - head_a) * (1.0 / A_HD) + RMS_EPS)
        out = (xt * jnp.where(first_head, inv_a, inv_b) * gain_ref[:, lanes]).astype(BF16)
        if j < A_PAIRS:
            q_ref[:, lanes] = out
        else:
            k_ref[:, j * 2 * A_HD - A_INNER:(j + 1) * 2 * A_HD - A_INNER] = out


def _attn_pre(x, g, w_in, gain):
    B, T, D = x.shape
    bm = min(APRE_BM, T)
    nt = T // bm
    full = lambda *shape: pl.BlockSpec(shape, lambda b, t: (0,) * len(shape))
    tok = lambda width: pl.BlockSpec((None, bm, width), lambda b, t: (b, t, 0))
    o = jax.ShapeDtypeStruct((B, T, A_INNER), BF16)
    return pl.pallas_call(
        _attn_pre_kernel,
        grid=(B, nt),
        in_specs=[tok(D), full(1, D), full(D, 4 * A_INNER), full(1, 2 * A_INNER)],
        out_specs=(tok(A_INNER),) * 4,
        out_shape=(o, o, o, o),
        scratch_shapes=[pltpu.VMEM((bm, 2 * A_INNER), F32)],
        compiler_params=pltpu.CompilerParams(
            dimension_semantics=("parallel", "parallel"), vmem_limit_bytes=VMEM_LIMIT),
        name="attn_pre",
    )(x, g, w_in, gain)


def _attn_kernel(q_ref, kp_ref, kc_ref, vp_ref, vc_ref, sz_ref, x_ref, bias_ref, wout_ref,
                 out_ref,
                 kcat, vcat, o_s, s_even, s_odd, m_even, m_odd):
    i = pl.program_id(1)
    bm = q_ref.shape[0]
    group = 2 * CHUNK
    n_groups = bm // group
    kcat[0:bm, :] = kp_ref[...]
    kcat[bm:2 * bm, :] = kc_ref[...]
    vcat[0:bm, :] = vp_ref[...]
    vcat[bm:2 * bm, :] = vc_ref[...]
    first_head = lax.broadcasted_iota(jnp.int32, (CHUNK, 2 * A_HD), 1) < A_HD
    key_off = lax.broadcasted_iota(jnp.int32, (1, BAND2), 1)
    ones_tile = jnp.ones((BAND2, LANES), BF16)

    lanes = lambda j: slice(j * 2 * A_HD, (j + 1) * 2 * A_HD)
    rows = lambda g: slice(g * group, (g + 1) * group)
    band = lambda g: slice(g * group + bm - LEFT_CHUNKS * CHUNK,
                           g * group + bm - LEFT_CHUNKS * CHUNK + BAND2)

    def scores(masked, g, j, s_ref, m_ref):
        q4 = q_ref[rows(g), lanes(j)].astype(F32)
        parts = []
        for c in range(2):
            qc = q4[c * CHUNK:(c + 1) * CHUNK, :]
            parts += [jnp.where(first_head, qc, 0.0), jnp.where(first_head, 0.0, qc)]
        wt = jnp.concatenate(parts, axis=0).astype(BF16)
        s = lax.dot_general(wt, kcat[band(g), lanes(j)], NT_DIMS,
                            preferred_element_type=F32) + bias_ref[j]
        if masked:
            valid = (g * group - LEFT_CHUNKS * CHUNK + key_off) >= 0
            s = jnp.where(valid, s, -jnp.inf)
        s_ref[j] = s
        m_ref[j] = jnp.broadcast_to(jnp.max(s, axis=1, keepdims=True), (2 * group, LANES))

    def finish(g, j, s_ref, m_ref):
        m = m_ref[j]
        e = jnp.exp2(s_ref[j] - jnp.concatenate([m] * (BAND2 // LANES), axis=1))
        v2 = jnp.concatenate([vcat[band(g), lanes(j)], ones_tile], axis=1)
        r = jnp.dot(e.astype(BF16), v2, preferred_element_type=F32)
        r = r[:, 0:2 * A_HD] * (1.0 / r[:, 2 * A_HD:4 * A_HD])
        o4 = jnp.concatenate(
            [jnp.where(first_head, r[2 * c * CHUNK:(2 * c + 1) * CHUNK, :],
                       r[(2 * c + 1) * CHUNK:(2 * c + 2) * CHUNK, :]) for c in range(2)], axis=0)
        o_s[rows(g), lanes(j)] = (o4 * sz_ref[rows(g), lanes(j)].astype(F32)).astype(BF16)

    even, odd = (s_even, m_even), (s_odd, m_odd)

    def run(masked):
        for j in range(A_PAIRS):
            scores(masked, 0, j, *even)
        for g in range(n_groups):
            cur, nxt = (even, odd) if g % 2 == 0 else (odd, even)
            for j in range(A_PAIRS):
                finish(g, j, *cur)
                if g + 1 < n_groups:
                    scores(masked, g + 1, j, *nxt)

    @pl.when(i == 0)
    def _():
        run(True)

    @pl.when(i > 0)
    def _():
        run(False)

    out_ref[...] = x_ref[...] + jnp.dot(o_s[...], wout_ref[...], preferred_element_type=F32)


def _attn(q, k, v, sz, x, bias, w_out):
    B, T, D = x.shape
    bm = ATT_BM
    nt = T // bm
    once = lambda *shape: pl.BlockSpec(shape, lambda b, t: (0,) * len(shape),
                                       pipeline_mode=pl.Buffered(1))
    cur = lambda width: pl.BlockSpec((None, bm, width), lambda b, t: (b, t, 0))
    prev = lambda width: pl.BlockSpec((None, bm, width), lambda b, t: (b, jnp.maximum(t - 1, 0), 0))
    return pl.pallas_call(
        _attn_kernel,
        grid=(B, nt),
        in_specs=[cur(A_INNER), prev(A_INNER), cur(A_INNER), prev(A_INNER), cur(A_INNER),
                  cur(A_INNER), cur(D), once(A_PAIRS, 4 * CHUNK, BAND2), once(A_INNER, D)],
        out_specs=cur(D),
        out_shape=jax.ShapeDtypeStruct((B, T, D), F32),
        scratch_shapes=[
            pltpu.VMEM((2 * bm, A_INNER), BF16),
            pltpu.VMEM((2 * bm, A_INNER), BF16),
            pltpu.VMEM((bm, A_INNER), BF16),
            pltpu.VMEM((A_PAIRS, 4 * CHUNK, BAND2), F32),
            pltpu.VMEM((A_PAIRS, 4 * CHUNK, BAND2), F32),
            pltpu.VMEM((A_PAIRS, 4 * CHUNK, LANES), F32),
            pltpu.VMEM((A_PAIRS, 4 * CHUNK, LANES), F32),
        ],
        compiler_params=pltpu.CompilerParams(
            dimension_semantics=("parallel", "parallel"), vmem_limit_bytes=VMEM_LIMIT),
        name="attn",
    )(q, k, k, v, v, sz, x, bias, w_out)


def kernel(x, norm_g, a_w_in, a_conv_w, a_conv_b, a_w_q, a_w_k, a_w_v, a_w_o, a_w_if, a_b_if,
           a_out_g, a_skip, a_w_out, b_w_in, b_q_g, b_k_g, b_rel_bias, b_w_out):
    assert x.shape[1] % ATT_BM == 0 and x.shape[2] == D_MODEL
    bf = lambda w: w.astype(BF16)
    pad_gate = lambda w: jnp.pad(w, ((0, 0), (0, GATE_PAD - 2 * M_HEADS)))

    x1 = _mlstm_layer(
        x, norm_g[0][None, :], bf(a_w_in[0]), a_conv_w[0], a_conv_b[0][None, :],
        bf(a_w_q[0]), bf(a_w_k[0]), bf(a_w_v[0]), bf(a_w_o[0]),
        bf(pad_gate(a_w_if[0])), pad_gate(a_b_if[0][None, :]), a_skip[0][None, :],
        a_out_g[0].reshape(1, M_INNER), bf(a_w_out[0]))

    bias = _attn_bias(b_rel_bias[0])
    gain = jnp.concatenate([jnp.tile(b_q_g[0], A_HEADS) * (A_HD ** -0.5 * LOG2_E),
                            jnp.tile(b_k_g[0], A_HEADS)])[None, :]
    q, k, v, sz = _attn_pre(x1, norm_g[1][None, :], bf(b_w_in[0]), gain)
    return _attn(q, k, v, sz, x1, bias, bf(b_w_out[0]))
```

```python
import functools

import jax
import jax.numpy as jnp
from jax import lax
from jax.experimental import pallas as pl
from jax.experimental.pallas import tpu as pltpu

F32 = jnp.float32
BF16 = jnp.bfloat16

RMS_EPS = 1e-6
CHUNK = 64
LOG2_E = 1.4426950408889634

D_MODEL = 1024
M_INNER = 2 * D_MODEL
M_HEADS = 4
M_DV = M_INNER // M_HEADS
M_DQK = M_DV // 2
CONV_K = 4

A_INNER = D_MODEL
A_HEADS = 16
A_HD = A_INNER // A_HEADS
A_PAIRS = A_HEADS // 2
LEFT_CHUNKS = 8
BAND = (LEFT_CHUNKS + 1) * CHUNK
REL_PAST = 256
REL_FUTURE = CHUNK - 1
REL_SIZE = REL_PAST + REL_FUTURE + 1

LANES = 128
SUBLANES = 8
GATE_PAD = LANES
VMEM_LIMIT = 56 * 1024 * 1024
FUSED_VMEM_LIMIT = 60 * 1024 * 1024
REC_L = 256
APRE_BM = 256
ATT_BM = LEFT_CHUNKS * CHUNK
BAND2 = BAND + CHUNK
BIAS_W = BAND2

NT_DIMS = (((1,), (1,)), ((), ()))
TN_DIMS = (((0,), (0,)), ((), ()))


def _sigmoid(x):
    return 0.5 * jnp.tanh(0.5 * x) + 0.5


def _log_sigmoid(x):
    return jnp.minimum(x, 0.0) - jnp.log1p(jnp.exp(-jnp.abs(x)))


def _rms(x, eps=RMS_EPS):
    return x * lax.rsqrt(jnp.mean(x * x, axis=-1, keepdims=True) + eps)


def _mlstm_layer_kernel(xp_ref, xc_ref, g_ref, win_ref, cw_ref, cb_ref, wq_ref, wk_ref, wv_ref, wo_ref,
                        wif_ref, bif_ref, skip_ref, og_ref, wout_ref,
                        out_ref,
                        ubuf, cb_scr, hq, hk, hv, ho, hcs, hsz, hgc, hgr,
                        ctx_ref, ctxb_ref, m_ref, s_scr, qc_scr, p_scr, kw_scr, pv_scr, cu_scr, y_scr,
                        *, blocks_per_row, n_blocks):
    step = pl.program_id(0)
    L = xp_ref.shape[0]
    wr = step % 2
    rd = 1 - wr
    prod_t = jnp.minimum(step, n_blocks - 1) % blocks_per_row
    cons_t = jnp.maximum(step - 1, 0) % blocks_per_row
    heads = range(M_HEADS)
    dv = lambda h: slice(h * M_DV, (h + 1) * M_DV)
    qk = lambda h: slice(h * M_DQK, (h + 1) * M_DQK)

    @pl.when(step == 0)
    def _():
        for ref in (hq, hk, hv, ho, hcs, hsz, hgc, hgr):
            ref[1] = jnp.zeros(ref.shape[1:], ref.dtype)

    @pl.when(prod_t == 0)
    def _():
        ubuf[0:SUBLANES, :] = jnp.zeros((SUBLANES, M_INNER), F32)

    @pl.when(cons_t == 0)
    def _():
        ctx_ref[...] = jnp.zeros(ctx_ref.shape, F32)
        ctxb_ref[...] = jnp.zeros(ctxb_ref.shape, BF16)
        m_ref[...] = jnp.zeros(m_ref.shape, F32)

    for h in heads:
        qh = hq[rd, :, qk(h)]
        s_scr[h] = lax.dot_general(qh, hk[rd, :, qk(h)], NT_DIMS, preferred_element_type=F32)
        qc_scr[h] = jnp.dot(qh, ctxb_ref[h], preferred_element_type=F32)

    xn = (_rms(xp_ref[...]) * g_ref[...]).astype(BF16)

    gc = hgc[rd]
    b = gc
    ar = pltpu.roll(gc, GATE_PAD - M_HEADS, 1)
    cm = pltpu.roll(gc, GATE_PAD - 2 * M_HEADS, 1)
    m_prev = m_ref[...]
    g_last = b[L - 1:L, :]
    a_c = g_last + ar
    m_new = jnp.maximum(g_last + m_prev, jnp.max(a_c, axis=0, keepdims=True))
    w_all = jnp.exp(a_c - m_new)
    decay = jnp.exp(g_last + m_prev - m_new)
    mx = jnp.maximum(m_prev, cm)
    inter_all = jnp.exp(m_prev - mx)
    enmt_all = jnp.exp(-(b + mx))
    m_ref[...] = m_new
    gr = hgr[rd]
    tril = (lax.broadcasted_iota(jnp.int32, (L, L), 1)
            <= lax.broadcasted_iota(jnp.int32, (L, L), 0))

    for h in heads:
        d = jnp.where(tril, jnp.exp(gr[h:h + 1, :] - mx[:, h:h + 1]), 0.0)
        p_scr[h] = (s_scr[h] * d).astype(BF16)
        kw_scr[h] = (hk[rd, :, qk(h)].astype(F32) * w_all[:, h:h + 1]).astype(BF16)

    for h in heads:
        ubuf[SUBLANES:SUBLANES + L, dv(h)] = jnp.dot(xn, win_ref[:, dv(h)],
                                                     preferred_element_type=F32)

    ones_tile = jnp.ones((L, LANES), BF16)
    vx = lambda h: jnp.concatenate([hv[rd, :, dv(h)], ones_tile], axis=1)

    def numerators_and_increments():
        for h in heads:
            pv_scr[h] = jnp.dot(p_scr[h], vx(h), preferred_element_type=F32)
            cu_scr[h] = lax.dot_general(kw_scr[h], vx(h), TN_DIMS, preferred_element_type=F32)

    def conv_branch(h):
        u = ubuf[SUBLANES:SUBLANES + L, dv(h)]
        conv = cb_ref[:, dv(h)] + cw_ref[CONV_K - 1:CONV_K, dv(h)] * u
        for j in range(CONV_K - 1):
            shifted = ubuf[pl.ds(SUBLANES - (CONV_K - 1) + j, L), dv(h)]
            conv = conv + cw_ref[j:j + 1, dv(h)] * shifted
        c = conv * _sigmoid(conv)
        cb_scr[:, dv(h)] = c.astype(BF16)
        hcs[wr, :, dv(h)] = (skip_ref[:, dv(h)] * c).astype(BF16)

    def value_projections(h):
        ub16 = ubuf[SUBLANES:SUBLANES + L, dv(h)].astype(BF16)
        hv[wr, :, dv(h)] = jnp.dot(ub16, wv_ref[h], preferred_element_type=F32).astype(BF16)
        ho[wr, :, dv(h)] = _sigmoid(jnp.dot(ub16, wo_ref[h], preferred_element_type=F32)).astype(BF16)

    def key_projections(h):
        cb16 = cb_scr[:, dv(h)]
        hq[wr, :, qk(h)] = jnp.dot(cb16, wq_ref[h], preferred_element_type=F32).astype(BF16)
        hk[wr, :, qk(h)] = (jnp.dot(cb16, wk_ref[h], preferred_element_type=F32)
                            * (M_DQK ** -0.5)).astype(BF16)

    def gate_branch(h):
        z = jnp.dot(xn, win_ref[:, M_INNER + h * M_DV:M_INNER + (h + 1) * M_DV],
                    preferred_element_type=F32)
        hsz[wr, :, dv(h)] = (z * _sigmoid(z)).astype(BF16)

    def normalise(h):
        inter = jnp.broadcast_to(inter_all[:, h:h + 1], (L, LANES))
        enmt = jnp.broadcast_to(enmt_all[:, h:h + 1], (L, LANES))
        num = pv_scr[h, :, 0:M_DV] + jnp.tile(inter, (1, M_DV // LANES)) * qc_scr[h, :, 0:M_DV]
        den = pv_scr[h, :, M_DV:M_DV + LANES] + inter * qc_scr[h, :, M_DV:M_DV + LANES]
        r = 1.0 / jnp.maximum(jnp.abs(den), enmt)
        ms = jnp.mean(num * num, axis=1, keepdims=True)
        scale = r * lax.rsqrt(r * r * ms + RMS_EPS)
        hn = (num * jnp.tile(scale, (1, M_DV // LANES)) * og_ref[:, dv(h)]).astype(BF16)
        y_scr[:, dv(h)] = (ho[rd, :, dv(h)] * hn + hcs[rd, :, dv(h)]) * hsz[rd, :, dv(h)]

    def project(h):
        part = jnp.dot(y_scr[:, dv(h)], wout_ref[dv(h), :], preferred_element_type=F32)
        if h == 0:
            out_ref[...] = xc_ref[...] + part
        else:
            out_ref[...] += part

    for h in heads:
        conv_branch(h)
    ubuf[0:SUBLANES, :] = ubuf[L:L + SUBLANES, :]
    numerators_and_increments()

    def gate_algebra():
        gates = bif_ref[...] + jnp.dot(cb_scr[...], wif_ref[...], preferred_element_type=F32)
        lf = _log_sigmoid(gates)
        lf_hi = lf.astype(BF16)
        rem = lf - lf_hi.astype(F32)
        lf_mid = rem.astype(BF16)
        lf_lo = (rem - lf_mid.astype(F32)).astype(BF16)
        trilb = tril.astype(BF16)
        cum = (jnp.dot(trilb, lf_hi, preferred_element_type=F32)
               + jnp.dot(trilb, lf_mid, preferred_element_type=F32)
               + jnp.dot(trilb, lf_lo, preferred_element_type=F32))
        nb = pltpu.roll(cum, GATE_PAD - M_HEADS, 1)
        nar = gates - nb
        row = lax.broadcasted_iota(jnp.int32, (L, GATE_PAD), 0)
        ncm = nar
        shift = 1
        while shift < L:
            ncm = jnp.maximum(ncm, jnp.where(row >= shift, pltpu.roll(ncm, shift, 0), -jnp.inf))
            shift *= 2
        lane = lax.broadcasted_iota(jnp.int32, (L, GATE_PAD), 1)
        hgc[wr] = jnp.where(lane < M_HEADS, nb,
                            jnp.where(lane < 2 * M_HEADS, pltpu.roll(nar, M_HEADS, 1),
                                      pltpu.roll(ncm, 2 * M_HEADS, 1)))
        hgr[wr] = nar.T[0:2 * M_HEADS, :]

    for h in heads:
        if h == M_HEADS // 2:
            gate_algebra()
        gate_branch(h)
        value_projections(h)
        key_projections(h)
        normalise(h)
        if h > 0:
            project(h - 1)
    project(M_HEADS - 1)

    for h in heads:
        ctx = decay[:, h:h + 1] * ctx_ref[h] + cu_scr[h]
        ctx_ref[h] = ctx
        ctxb_ref[h] = ctx.astype(BF16)


def _mlstm_layer(x, g, w_in, conv_w, conv_b, w_q, w_k, w_v, w_o, w_if, b_if, skip, out_g, w_out):
    B, T, D = x.shape
    L = REC_L
    nt = T // L
    n_blocks = B * nt
    dvx = M_DV + LANES
    once = lambda *shape: pl.BlockSpec(shape, lambda s: (0,) * len(shape), pipeline_mode=pl.Buffered(1))

    def prod_map(s):
        blk = jnp.minimum(s, n_blocks - 1)
        return (blk // nt, blk % nt, 0)

    def cons_map(s):
        blk = jnp.maximum(s - 1, 0)
        return (blk // nt, blk % nt, 0)

    hand = lambda width, dtype: pltpu.VMEM((2, L, width), dtype)
    return pl.pallas_call(
        functools.partial(_mlstm_layer_kernel, blocks_per_row=nt, n_blocks=n_blocks),
        grid=(n_blocks + 1,),
        in_specs=[
            pl.BlockSpec((None, L, D), prod_map), pl.BlockSpec((None, L, D), cons_map),
            once(1, D), once(D, 2 * M_INNER), once(CONV_K, M_INNER), once(1, M_INNER),
            once(M_HEADS, M_DV, M_DQK), once(M_HEADS, M_DV, M_DQK),
            once(M_HEADS, M_DV, M_DV), once(M_HEADS, M_DV, M_DV),
            once(M_INNER, GATE_PAD), once(1, GATE_PAD), once(1, M_INNER),
            once(1, M_INNER), once(M_INNER, D),
        ],
        out_specs=pl.BlockSpec((None, L, D), cons_map),
        out_shape=jax.ShapeDtypeStruct((B, T, D), F32),
        scratch_shapes=[
            pltpu.VMEM((SUBLANES + L, M_INNER), F32),
            pltpu.VMEM((L, M_INNER), BF16),
            hand(M_HEADS * M_DQK, BF16), hand(M_HEADS * M_DQK, BF16),
            hand(M_INNER, BF16), hand(M_INNER, BF16),
            hand(M_INNER, BF16), hand(M_INNER, BF16),
            hand(GATE_PAD, F32), pltpu.VMEM((2, 2 * M_HEADS, L), F32),
            pltpu.VMEM((M_HEADS, M_DQK, dvx), F32),
            pltpu.VMEM((M_HEADS, M_DQK, dvx), BF16),
            pltpu.VMEM((1, LANES), F32),
            pltpu.VMEM((M_HEADS, L, L), F32),
            pltpu.VMEM((M_HEADS, L, dvx), F32),
            pltpu.VMEM((M_HEADS, L, L), BF16),
            pltpu.VMEM((M_HEADS, L, M_DQK), BF16),
            pltpu.VMEM((M_HEADS, L, dvx), F32),
            pltpu.VMEM((M_HEADS, M_DQK, dvx), F32),
            pltpu.VMEM((L, M_INNER), BF16),
        ],
        compiler_params=pltpu.CompilerParams(
            dimension_semantics=("arbitrary",), vmem_limit_bytes=FUSED_VMEM_LIMIT),
        name="mlstm_layer",
    )(x, x, g, w_in, conv_w, conv_b, w_q, w_k, w_v, w_o, w_if, b_if, skip, out_g, w_out)


def _bias_kernel(tab_ref, out_ref):
    r = lax.broadcasted_iota(jnp.int32, (REL_SIZE, BIAS_W), 0)
    e = lax.broadcasted_iota(jnp.int32, (REL_SIZE, BIAS_W), 1)
    e = jnp.where(e >= BAND, e - BIAS_W, e)
    idx = jnp.clip(LEFT_CHUNKS * CHUNK - e, -REL_FUTURE, REL_PAST) + REL_FUTURE
    onehot = (r == idx).astype(F32)
    g = jnp.dot(tab_ref[...], onehot, precision=lax.Precision.HIGHEST,
                preferred_element_type=F32) * LOG2_E
    for j in range(A_PAIRS):
        halves = []
        for a in range(2):
            rows = jnp.broadcast_to(g[2 * j + a:2 * j + a + 1, :], (CHUNK, BIAS_W))
            halves.append(pltpu.roll(rows, 0, 1, stride=1, stride_axis=0))
        both = jnp.concatenate(halves, axis=0)
        col = lax.broadcasted_iota(jnp.int32, (2 * CHUNK, BAND2), 1)
        first = jnp.where(col < BAND, both, -jnp.inf)
        second = jnp.where(col >= CHUNK, pltpu.roll(both, CHUNK, 1), -jnp.inf)
        out_ref[j] = jnp.concatenate([first, second], axis=0)


def _attn_bias(rel_bias):
    return pl.pallas_call(
        _bias_kernel,
        out_shape=jax.ShapeDtypeStruct((A_PAIRS, 4 * CHUNK, BAND2), F32),
        compiler_params=pltpu.CompilerParams(vmem_limit_bytes=VMEM_LIMIT),
        name="attn_bias",
    )(rel_bias)


def _attn_pre_kernel(x_ref, g_ref, win_ref, gain_ref,
                     q_ref, k_ref, v_ref, sz_ref,
                     qk_scr):
    bm = x_ref.shape[0]
    xn = (_rms(x_ref[...]) * g_ref[...]).astype(BF16)
    qk_scr[...] = jnp.dot(xn, win_ref[:, 0:2 * A_INNER], preferred_element_type=F32)
    v_ref[...] = jnp.dot(xn, win_ref[:, 2 * A_INNER:3 * A_INNER],
                         preferred_element_type=F32).astype(BF16)
    z = jnp.dot(xn, win_ref[:, 3 * A_INNER:4 * A_INNER], preferred_element_type=F32)
    sz_ref[...] = (z * _sigmoid(z)).astype(BF16)

    first_head = lax.broadcasted_iota(jnp.int32, (bm, 2 * A_HD), 1) < A_HD
    for j in range(2 * A_PAIRS):
        lanes = slice(j * 2 * A_HD, (j + 1) * 2 * A_HD)
        xt = qk_scr[:, lanes]
        sq = xt * xt
        head_a = jnp.sum(jnp.where(first_head, sq, 0.0), axis=1, keepdims=True)
        head_b = jnp.sum(jnp.where(first_head, 0.0, sq), axis=1, keepdims=True)
        inv_a = lax.rsqrt(head_a * (1.0 / A_HD) + RMS_EPS)
        inv_b = lax.rsqrt(head_b * (1.0 / A_HD) + RMS_EPS)
        out = (xt * jnp.where(first_head, inv_a, inv_b) * gain_ref[:, lanes]).astype(BF16)
        if j < A_PAIRS:
            q_ref[:, lanes] = out
        else:
            k_ref[:, j * 2 * A_HD - A_INNER:(j + 1) * 2 * A_HD - A_INNER] = out


def _attn_pre(x, g, w_in, gain):
    B, T, D = x.shape
    bm = min(APRE_BM, T)
    nt = T // bm
    full = lambda *shape: pl.BlockSpec(shape, lambda b, t: (0,) * len(shape))
    tok = lambda width: pl.BlockSpec((None, bm, width), lambda b, t: (b, t, 0))
    o = jax.ShapeDtypeStruct((B, T, A_INNER), BF16)
    return pl.pallas_call(
        _attn_pre_kernel,
        grid=(B, nt),
        in_specs=[tok(D), full(1, D), full(D, 4 * A_INNER), full(1, 2 * A_INNER)],
        out_specs=(tok(A_INNER),) * 4,
        out_shape=(o, o, o, o),
        scratch_shapes=[pltpu.VMEM((bm, 2 * A_INNER), F32)],
        compiler_params=pltpu.CompilerParams(
            dimension_semantics=("parallel", "parallel"), vmem_limit_bytes=VMEM_LIMIT),
        name="attn_pre",
    )(x, g, w_in, gain)


def _attn_kernel(q_ref, kp_ref, kc_ref, vp_ref, vc_ref, sz_ref, x_ref, bias_ref, wout_ref,
                 out_ref,
                 kcat, vcat, o_s, s_even, s_odd, m_even, m_odd):
    i = pl.program_id(1)
    bm = q_ref.shape[0]
    group = 2 * CHUNK
    n_groups = bm // group
    kcat[0:bm, :] = kp_ref[...]
    kcat[bm:2 * bm, :] = kc_ref[...]
    vcat[0:bm, :] = vp_ref[...]
    vcat[bm:2 * bm, :] = vc_ref[...]
    first_head = lax.broadcasted_iota(jnp.int32, (CHUNK, 2 * A_HD), 1) < A_HD
    key_off = lax.broadcasted_iota(jnp.int32, (1, BAND2), 1)
    ones_tile = jnp.ones((BAND2, LANES), BF16)

    lanes = lambda j: slice(j * 2 * A_HD, (j + 1) * 2 * A_HD)
    rows = lambda g: slice(g * group, (g + 1) * group)
    band = lambda g: slice(g * group + bm - LEFT_CHUNKS * CHUNK,
                           g * group + bm - LEFT_CHUNKS * CHUNK + BAND2)

    def scores(masked, g, j, s_ref, m_ref):
        q4 = q_ref[rows(g), lanes(j)].astype(F32)
        parts = []
        for c in range(2):
            qc = q4[c * CHUNK:(c + 1) * CHUNK, :]
            parts += [jnp.where(first_head, qc, 0.0), jnp.where(first_head, 0.0, qc)]
        wt = jnp.concatenate(parts, axis=0).astype(BF16)
        s = lax.dot_general(wt, kcat[band(g), lanes(j)], NT_DIMS,
                            preferred_element_type=F32) + bias_ref[j]
        if masked:
            valid = (g * group - LEFT_CHUNKS * CHUNK + key_off) >= 0
            s = jnp.where(valid, s, -jnp.inf)
        s_ref[j] = s
        m_ref[j] = jnp.broadcast_to(jnp.max(s, axis=1, keepdims=True), (2 * group, LANES))

    def finish(g, j, s_ref, m_ref):
        m = m_ref[j]
        e = jnp.exp2(s_ref[j] - jnp.concatenate([m] * (BAND2 // LANES), axis=1))
        v2 = jnp.concatenate([vcat[band(g), lanes(j)], ones_tile], axis=1)
        r = jnp.dot(e.astype(BF16), v2, preferred_element_type=F32)
        r = r[:, 0:2 * A_HD] * (1.0 / r[:, 2 * A_HD:4 * A_HD])
        o4 = jnp.concatenate(
            [jnp.where(first_head, r[2 * c * CHUNK:(2 * c + 1) * CHUNK, :],
                       r[(2 * c + 1) * CHUNK:(2 * c + 2) * CHUNK, :]) for c in range(2)], axis=0)
        o_s[rows(g), lanes(j)] = (o4 * sz_ref[rows(g), lanes(j)].astype(F32)).astype(BF16)

    even, odd = (s_even, m_even), (s_odd, m_odd)

    def run(masked):
        for j in range(A_PAIRS):
            scores(masked, 0, j, *even)
        for g in range(n_groups):
            cur, nxt = (even, odd) if g % 2 == 0 else (odd, even)
            for j in range(A_PAIRS):
                finish(g, j, *cur)
                if g + 1 < n_groups:
                    scores(masked, g + 1, j, *nxt)

    @pl.when(i == 0)
    def _():
        run(True)

    @pl.when(i > 0)
    def _():
        run(False)

    out_ref[...] = x_ref[...] + jnp.dot(o_s[...], wout_ref[...], preferred_element_type=F32)


def _attn(q, k, v, sz, x, bias, w_out):
    B, T, D = x.shape
    bm = ATT_BM
    nt = T // bm
    once = lambda *shape: pl.BlockSpec(shape, lambda b, t: (0,) * len(shape),
                                       pipeline_mode=pl.Buffered(1))
    cur = lambda width: pl.BlockSpec((None, bm, width), lambda b, t: (b, t, 0))
    prev = lambda width: pl.BlockSpec((None, bm, width), lambda b, t: (b, jnp.maximum(t - 1, 0), 0))
    return pl.pallas_call(
        _attn_kernel,
        grid=(B, nt),
        in_specs=[cur(A_INNER), prev(A_INNER), cur(A_INNER), prev(A_INNER), cur(A_INNER),
                  cur(A_INNER), cur(D), once(A_PAIRS, 4 * CHUNK, BAND2), once(A_INNER, D)],
        out_specs=cur(D),
        out_shape=jax.ShapeDtypeStruct((B, T, D), F32),
        scratch_shapes=[
            pltpu.VMEM((2 * bm, A_INNER), BF16),
            pltpu.VMEM((2 * bm, A_INNER), BF16),
            pltpu.VMEM((bm, A_INNER), BF16),
            pltpu.VMEM((A_PAIRS, 4 * CHUNK, BAND2), F32),
            pltpu.VMEM((A_PAIRS, 4 * CHUNK, BAND2), F32),
            pltpu.VMEM((A_PAIRS, 4 * CHUNK, LANES), F32),
            pltpu.VMEM((A_PAIRS, 4 * CHUNK, LANES), F32),
        ],
        compiler_params=pltpu.CompilerParams(
            dimension_semantics=("parallel", "parallel"), vmem_limit_bytes=VMEM_LIMIT),
        name="attn",
    )(q, k, k, v, v, sz, x, bias, w_out)


def kernel(x, norm_g, a_w_in, a_conv_w, a_conv_b, a_w_q, a_w_k, a_w_v, a_w_o, a_w_if, a_b_if,
           a_out_g, a_skip, a_w_out, b_w_in, b_q_g, b_k_g, b_rel_bias, b_w_out):
    assert x.shape[1] % ATT_BM == 0 and x.shape[2] == D_MODEL
    bf = lambda w: w.astype(BF16)
    pad_gate = lambda w: jnp.pad(w, ((0, 0), (0, GATE_PAD - 2 * M_HEADS)))

    x1 = _mlstm_layer(
        x, norm_g[0][None, :], bf(a_w_in[0]), a_conv_w[0], a_conv_b[0][None, :],
        bf(a_w_q[0]), bf(a_w_k[0]), bf(a_w_v[0]), bf(a_w_o[0]),
        bf(pad_gate(a_w_if[0])), pad_gate(a_b_if[0][None, :]), a_skip[0][None, :],
        a_out_g[0].reshape(1, M_INNER), bf(a_w_out[0]))

    bias = _attn_bias(b_rel_bias[0])
    gain = jnp.concatenate([jnp.tile(b_q_g[0], A_HEADS) * (A_HD ** -0.5 * LOG2_E),
                            jnp.tile(b_k_g[0], A_HEADS)])[None, :]
    q, k, v, sz = _attn_pre(x1, norm_g[1][None, :], bf(b_w_in[0]), gain)
    return _attn(q, k, v, sz, x1, bias, bf(b_w_out[0]))
```

```python
import functools

import jax
import jax.numpy as jnp
from jax import lax
from jax.experimental import pallas as pl
from jax.experimental.pallas import tpu as pltpu

F32 = jnp.float32
BF16 = jnp.bfloat16

RMS_EPS = 1e-6
CHUNK = 64
LOG2_E = 1.4426950408889634

D_MODEL = 1024
M_INNER = 2 * D_MODEL
M_HEADS = 4
M_DV = M_INNER // M_HEADS
M_DQK = M_DV // 2
CONV_K = 4

A_INNER = D_MODEL
A_HEADS = 16
A_HD = A_INNER // A_HEADS
A_PAIRS = A_HEADS // 2
LEFT_CHUNKS = 8
BAND = (LEFT_CHUNKS + 1) * CHUNK
REL_PAST = 256
REL_FUTURE = CHUNK - 1
REL_SIZE = REL_PAST + REL_FUTURE + 1

LANES = 128
SUBLANES = 8
GATE_PAD = LANES
VMEM_LIMIT = 56 * 1024 * 1024
FUSED_VMEM_LIMIT = 60 * 1024 * 1024
REC_L = 256
APRE_BM = 512
ATT_BM = LEFT_CHUNKS * CHUNK
BAND2 = BAND + CHUNK
BIAS_W = BAND2

NT_DIMS = (((1,), (1,)), ((), ()))
TN_DIMS = (((0,), (0,)), ((), ()))


def _sigmoid(x):
    return 0.5 * jnp.tanh(0.5 * x) + 0.5


def _log_sigmoid(x):
    return jnp.minimum(x, 0.0) - jnp.log1p(jnp.exp(-jnp.abs(x)))


def _rms(x, eps=RMS_EPS):
    return x * lax.rsqrt(jnp.mean(x * x, axis=-1, keepdims=True) + eps)


def _mlstm_layer_kernel(xp_ref, xc_ref, g_ref, win_ref, cw_ref, cb_ref, wq_ref, wk_ref, wv_ref, wo_ref,
                        wif_ref, bif_ref, skip_ref, og_ref, wout_ref,
                        out_ref,
                        ubuf, cb_scr, hq, hk, hv, ho, hcs, hsz, hgc, hgr,
                        ctx_ref, ctxb_ref, m_ref, s_scr, qc_scr, p_scr, kw_scr, pv_scr, cu_scr, y_scr,
                        *, blocks_per_row, n_blocks):
    step = pl.program_id(0)
    L = xp_ref.shape[0]
    wr = step % 2
    rd = 1 - wr
    prod_t = jnp.minimum(step, n_blocks - 1) % blocks_per_row
    cons_t = jnp.maximum(step - 1, 0) % blocks_per_row
    heads = range(M_HEADS)
    dv = lambda h: slice(h * M_DV, (h + 1) * M_DV)
    qk = lambda h: slice(h * M_DQK, (h + 1) * M_DQK)

    @pl.when(step == 0)
    def _():
        for ref in (hq, hk, hv, ho, hcs, hsz, hgc, hgr):
            ref[1] = jnp.zeros(ref.shape[1:], ref.dtype)

    @pl.when(prod_t == 0)
    def _():
        ubuf[0:SUBLANES, :] = jnp.zeros((SUBLANES, M_INNER), F32)

    @pl.when(cons_t == 0)
    def _():
        ctx_ref[...] = jnp.zeros(ctx_ref.shape, F32)
        ctxb_ref[...] = jnp.zeros(ctxb_ref.shape, BF16)
        m_ref[...] = jnp.zeros(m_ref.shape, F32)

    for h in heads:
        qh = hq[rd, :, qk(h)]
        s_scr[h] = lax.dot_general(qh, hk[rd, :, qk(h)], NT_DIMS, preferred_element_type=F32)
        qc_scr[h] = jnp.dot(qh, ctxb_ref[h], preferred_element_type=F32)

    xn = (_rms(xp_ref[...]) * g_ref[...]).astype(BF16)

    gc = hgc[rd]
    b = gc
    ar = pltpu.roll(gc, GATE_PAD - M_HEADS, 1)
    cm = pltpu.roll(gc, GATE_PAD - 2 * M_HEADS, 1)
    m_prev = m_ref[...]
    g_last = b[L - 1:L, :]
    a_c = g_last + ar
    m_new = jnp.maximum(g_last + m_prev, jnp.max(a_c, axis=0, keepdims=True))
    w_all = jnp.exp(a_c - m_new)
    decay = jnp.exp(g_last + m_prev - m_new)
    mx = jnp.maximum(m_prev, cm)
    inter_all = jnp.exp(m_prev - mx)
    enmt_all = jnp.exp(-(b + mx))
    m_ref[...] = m_new
    gr = hgr[rd]
    tril = (lax.broadcasted_iota(jnp.int32, (L, L), 1)
            <= lax.broadcasted_iota(jnp.int32, (L, L), 0))

    for h in heads:
        d = jnp.where(tril, jnp.exp(gr[h:h + 1, :] - mx[:, h:h + 1]), 0.0)
        p_scr[h] = (s_scr[h] * d).astype(BF16)
        kw_scr[h] = (hk[rd, :, qk(h)].astype(F32) * w_all[:, h:h + 1]).astype(BF16)

    for h in heads:
        ubuf[SUBLANES:SUBLANES + L, dv(h)] = jnp.dot(xn, win_ref[:, dv(h)],
                                                     preferred_element_type=F32)

    ones_tile = jnp.ones((L, LANES), BF16)
    vx = lambda h: jnp.concatenate([hv[rd, :, dv(h)], ones_tile], axis=1)

    def numerators_and_increments():
        for h in heads:
            pv_scr[h] = jnp.dot(p_scr[h], vx(h), preferred_element_type=F32)
            cu_scr[h] = lax.dot_general(kw_scr[h], vx(h), TN_DIMS, preferred_element_type=F32)

    def conv_branch(h):
        u = ubuf[SUBLANES:SUBLANES + L, dv(h)]
        conv = cb_ref[:, dv(h)] + cw_ref[CONV_K - 1:CONV_K, dv(h)] * u
        for j in range(CONV_K - 1):
            shifted = ubuf[pl.ds(SUBLANES - (CONV_K - 1) + j, L), dv(h)]
            conv = conv + cw_ref[j:j + 1, dv(h)] * shifted
        c = conv * _sigmoid(conv)
        cb_scr[:, dv(h)] = c.astype(BF16)
        hcs[wr, :, dv(h)] = (skip_ref[:, dv(h)] * c).astype(BF16)

    def value_projections(h):
        ub16 = ubuf[SUBLANES:SUBLANES + L, dv(h)].astype(BF16)
        hv[wr, :, dv(h)] = jnp.dot(ub16, wv_ref[h], preferred_element_type=F32).astype(BF16)
        ho[wr, :, dv(h)] = _sigmoid(jnp.dot(ub16, wo_ref[h], preferred_element_type=F32)).astype(BF16)

    def key_projections(h):
        cb16 = cb_scr[:, dv(h)]
        hq[wr, :, qk(h)] = jnp.dot(cb16, wq_ref[h], preferred_element_type=F32).astype(BF16)
        hk[wr, :, qk(h)] = (jnp.dot(cb16, wk_ref[h], preferred_element_type=F32)
                            * (M_DQK ** -0.5)).astype(BF16)

    def gate_branch(h):
        z = jnp.dot(xn, win_ref[:, M_INNER + h * M_DV:M_INNER + (h + 1) * M_DV],
                    preferred_element_type=F32)
        hsz[wr, :, dv(h)] = (z * _sigmoid(z)).astype(BF16)

    def normalise(h):
        inter = jnp.broadcast_to(inter_all[:, h:h + 1], (L, LANES))
        enmt = jnp.broadcast_to(enmt_all[:, h:h + 1], (L, LANES))
        num = pv_scr[h, :, 0:M_DV] + jnp.tile(inter, (1, M_DV // LANES)) * qc_scr[h, :, 0:M_DV]
        den = pv_scr[h, :, M_DV:M_DV + LANES] + inter * qc_scr[h, :, M_DV:M_DV + LANES]
        r = 1.0 / jnp.maximum(jnp.abs(den), enmt)
        ms = jnp.mean(num * num, axis=1, keepdims=True)
        scale = r * lax.rsqrt(r * r * ms + RMS_EPS)
        hn = (num * jnp.tile(scale, (1, M_DV // LANES)) * og_ref[:, dv(h)]).astype(BF16)
        y_scr[:, dv(h)] = (ho[rd, :, dv(h)] * hn + hcs[rd, :, dv(h)]) * hsz[rd, :, dv(h)]

    def project(h):
        part = jnp.dot(y_scr[:, dv(h)], wout_ref[dv(h), :], preferred_element_type=F32)
        if h == 0:
            out_ref[...] = xc_ref[...] + part
        else:
            out_ref[...] += part

    for h in heads:
        conv_branch(h)
    ubuf[0:SUBLANES, :] = ubuf[L:L + SUBLANES, :]
    numerators_and_increments()

    def gate_algebra():
        gates = bif_ref[...] + jnp.dot(cb_scr[...], wif_ref[...], preferred_element_type=F32)
        lf = _log_sigmoid(gates)
        lf_hi = lf.astype(BF16)
        rem = lf - lf_hi.astype(F32)
        lf_mid = rem.astype(BF16)
        lf_lo = (rem - lf_mid.astype(F32)).astype(BF16)
        trilb = tril.astype(BF16)
        cum = (jnp.dot(trilb, lf_hi, preferred_element_type=F32)
               + jnp.dot(trilb, lf_mid, preferred_element_type=F32)
               + jnp.dot(trilb, lf_lo, preferred_element_type=F32))
        nb = pltpu.roll(cum, GATE_PAD - M_HEADS, 1)
        nar = gates - nb
        row = lax.broadcasted_iota(jnp.int32, (L, GATE_PAD), 0)
        ncm = nar
        shift = 1
        while shift < L:
            ncm = jnp.maximum(ncm, jnp.where(row >= shift, pltpu.roll(ncm, shift, 0), -jnp.inf))
            shift *= 2
        lane = lax.broadcasted_iota(jnp.int32, (L, GATE_PAD), 1)
        hgc[wr] = jnp.where(lane < M_HEADS, nb,
                            jnp.where(lane < 2 * M_HEADS, pltpu.roll(nar, M_HEADS, 1),
                                      pltpu.roll(ncm, 2 * M_HEADS, 1)))
        hgr[wr] = nar.T[0:2 * M_HEADS, :]

    for h in heads:
        if h == M_HEADS // 2:
            gate_algebra()
        gate_branch(h)
        value_projections(h)
        key_projections(h)
        normalise(h)
        if h > 0:
            project(h - 1)
    project(M_HEADS - 1)

    for h in heads:
        ctx = decay[:, h:h + 1] * ctx_ref[h] + cu_scr[h]
        ctx_ref[h] = ctx
        ctxb_ref[h] = ctx.astype(BF16)


def _mlstm_layer(x, g, w_in, conv_w, conv_b, w_q, w_k, w_v, w_o, w_if, b_if, skip, out_g, w_out):
    B, T, D = x.shape
    L = REC_L
    nt = T // L
    n_blocks = B * nt
    dvx = M_DV + LANES
    once = lambda *shape: pl.BlockSpec(shape, lambda s: (0,) * len(shape), pipeline_mode=pl.Buffered(1))

    def prod_map(s):
        blk = jnp.minimum(s, n_blocks - 1)
        return (blk // nt, blk % nt, 0)

    def cons_map(s):
        blk = jnp.maximum(s - 1, 0)
        return (blk // nt, blk % nt, 0)

    hand = lambda width, dtype: pltpu.VMEM((2, L, width), dtype)
    return pl.pallas_call(
        functools.partial(_mlstm_layer_kernel, blocks_per_row=nt, n_blocks=n_blocks),
        grid=(n_blocks + 1,),
        in_specs=[
            pl.BlockSpec((None, L, D), prod_map), pl.BlockSpec((None, L, D), cons_map),
            once(1, D), once(D, 2 * M_INNER), once(CONV_K, M_INNER), once(1, M_INNER),
            once(M_HEADS, M_DV, M_DQK), once(M_HEADS, M_DV, M_DQK),
            once(M_HEADS, M_DV, M_DV), once(M_HEADS, M_DV, M_DV),
            once(M_INNER, GATE_PAD), once(1, GATE_PAD), once(1, M_INNER),
            once(1, M_INNER), once(M_INNER, D),
        ],
        out_specs=pl.BlockSpec((None, L, D), cons_map),
        out_shape=jax.ShapeDtypeStruct((B, T, D), F32),
        scratch_shapes=[
            pltpu.VMEM((SUBLANES + L, M_INNER), F32),
            pltpu.VMEM((L, M_INNER), BF16),
            hand(M_HEADS * M_DQK, BF16), hand(M_HEADS * M_DQK, BF16),
            hand(M_INNER, BF16), hand(M_INNER, BF16),
            hand(M_INNER, BF16), hand(M_INNER, BF16),
            hand(GATE_PAD, F32), pltpu.VMEM((2, 2 * M_HEADS, L), F32),
            pltpu.VMEM((M_HEADS, M_DQK, dvx), F32),
            pltpu.VMEM((M_HEADS, M_DQK, dvx), BF16),
            pltpu.VMEM((1, LANES), F32),
            pltpu.VMEM((M_HEADS, L, L), F32),
            pltpu.VMEM((M_HEADS, L, dvx), F32),
            pltpu.VMEM((M_HEADS, L, L), BF16),
            pltpu.VMEM((M_HEADS, L, M_DQK), BF16),
            pltpu.VMEM((M_HEADS, L, dvx), F32),
            pltpu.VMEM((M_HEADS, M_DQK, dvx), F32),
            pltpu.VMEM((L, M_INNER), BF16),
        ],
        compiler_params=pltpu.CompilerParams(
            dimension_semantics=("arbitrary",), vmem_limit_bytes=FUSED_VMEM_LIMIT),
        name="mlstm_layer",
    )(x, x, g, w_in, conv_w, conv_b, w_q, w_k, w_v, w_o, w_if, b_if, skip, out_g, w_out)


def _bias_kernel(tab_ref, out_ref):
    r = lax.broadcasted_iota(jnp.int32, (REL_SIZE, BIAS_W), 0)
    e = lax.broadcasted_iota(jnp.int32, (REL_SIZE, BIAS_W), 1)
    e = jnp.where(e >= BAND, e - BIAS_W, e)
    idx = jnp.clip(LEFT_CHUNKS * CHUNK - e, -REL_FUTURE, REL_PAST) + REL_FUTURE
    onehot = (r == idx).astype(F32)
    g = jnp.dot(tab_ref[...], onehot, precision=lax.Precision.HIGHEST,
                preferred_element_type=F32) * LOG2_E
    for j in range(A_PAIRS):
        halves = []
        for a in range(2):
            rows = jnp.broadcast_to(g[2 * j + a:2 * j + a + 1, :], (CHUNK, BIAS_W))
            halves.append(pltpu.roll(rows, 0, 1, stride=1, stride_axis=0))
        both = jnp.concatenate(halves, axis=0)
        col = lax.broadcasted_iota(jnp.int32, (2 * CHUNK, BAND2), 1)
        first = jnp.where(col < BAND, both, -jnp.inf)
        second = jnp.where(col >= CHUNK, pltpu.roll(both, CHUNK, 1), -jnp.inf)
        out_ref[j] = jnp.concatenate([first, second], axis=0)


def _attn_bias(rel_bias):
    return pl.pallas_call(
        _bias_kernel,
        out_shape=jax.ShapeDtypeStruct((A_PAIRS, 4 * CHUNK, BAND2), F32),
        compiler_params=pltpu.CompilerParams(vmem_limit_bytes=VMEM_LIMIT),
        name="attn_bias",
    )(rel_bias)


def _attn_pre_kernel(x_ref, g_ref, win_ref, gain_ref,
                     q_ref, k_ref, v_ref, sz_ref,
                     qk_scr):
    bm = x_ref.shape[0]
    xn = (_rms(x_ref[...]) * g_ref[...]).astype(BF16)
    qk_scr[...] = jnp.dot(xn, win_ref[:, 0:2 * A_INNER], preferred_element_type=F32)
    v_ref[...] = jnp.dot(xn, win_ref[:, 2 * A_INNER:3 * A_INNER],
                         preferred_element_type=F32).astype(BF16)
    z = jnp.dot(xn, win_ref[:, 3 * A_INNER:4 * A_INNER], preferred_element_type=F32)
    sz_ref[...] = (z * _sigmoid(z)).astype(BF16)

    first_head = lax.broadcasted_iota(jnp.int32, (bm, 2 * A_HD), 1) < A_HD
    for j in range(2 * A_PAIRS):
        lanes = slice(j * 2 * A_HD, (j + 1) * 2 * A_HD)
        xt = qk_scr[:, lanes]
        sq = xt * xt
        head_a = jnp.sum(jnp.where(first_head, sq, 0.0), axis=1, keepdims=True)
        head_b = jnp.sum(jnp.where(first_head, 0.0, sq), axis=1, keepdims=True)
        inv_a = lax.rsqrt(head_a * (1.0 / A_HD) + RMS_EPS)
        inv_b = lax.rsqrt(head_b * (1.0 / A_HD) + RMS_EPS)
        out = (xt * jnp.where(first_head, inv_a, inv_b) * gain_ref[:, lanes]).astype(BF16)
        if j < A_PAIRS:
            q_ref[:, lanes] = out
        else:
            k_ref[:, j * 2 * A_HD - A_INNER:(j + 1) * 2 * A_HD - A_INNER] = out


def _attn_pre(x, g, w_in, gain):
    B, T, D = x.shape
    bm = min(APRE_BM, T)
    nt = T // bm
    full = lambda *shape: pl.BlockSpec(shape, lambda b, t: (0,) * len(shape))
    tok = lambda width: pl.BlockSpec((None, bm, width), lambda b, t: (b, t, 0))
    o = jax.ShapeDtypeStruct((B, T, A_INNER), BF16)
    return pl.pallas_call(
        _attn_pre_kernel,
        grid=(B, nt),
        in_specs=[tok(D), full(1, D), full(D, 4 * A_INNER), full(1, 2 * A_INNER)],
        out_specs=(tok(A_INNER),) * 4,
        out_shape=(o, o, o, o),
        scratch_shapes=[pltpu.VMEM((bm, 2 * A_INNER), F32)],
        compiler_params=pltpu.CompilerParams(
            dimension_semantics=("parallel", "parallel"), vmem_limit_bytes=VMEM_LIMIT),
        name="attn_pre",
    )(x, g, w_in, gain)


def _attn_kernel(q_ref, kp_ref, kc_ref, vp_ref, vc_ref, sz_ref, x_ref, bias_ref, wout_ref,
                 out_ref,
                 o_s, s_even, s_odd, m_even, m_odd):
    i = pl.program_id(1)
    bm = q_ref.shape[0]
    group = 2 * CHUNK
    n_groups = bm // group
    first_head = lax.broadcasted_iota(jnp.int32, (CHUNK, 2 * A_HD), 1) < A_HD
    key_off = lax.broadcasted_iota(jnp.int32, (1, BAND2), 1)
    ones_tile = jnp.ones((bm, LANES), BF16)

    lanes = lambda j: slice(j * 2 * A_HD, (j + 1) * 2 * A_HD)
    rows = lambda g: slice(g * group, (g + 1) * group)
    old = lambda g: slice(g * group + bm - LEFT_CHUNKS * CHUNK, bm)
    new = lambda g: slice(0, (g + 1) * group)
    n_old = lambda g: LEFT_CHUNKS * CHUNK - g * group

    def scores(masked, g, j, s_ref, m_ref):
        q4 = q_ref[rows(g), lanes(j)].astype(F32)
        parts = []
        for c in range(2):
            qc = q4[c * CHUNK:(c + 1) * CHUNK, :]
            parts += [jnp.where(first_head, qc, 0.0), jnp.where(first_head, 0.0, qc)]
        wt = jnp.concatenate(parts, axis=0).astype(BF16)
        s = jnp.concatenate(
            [lax.dot_general(wt, kp_ref[old(g), lanes(j)], NT_DIMS, preferred_element_type=F32),
             lax.dot_general(wt, kc_ref[new(g), lanes(j)], NT_DIMS, preferred_element_type=F32)],
            axis=1) + bias_ref[j]
        if masked:
            valid = (g * group - LEFT_CHUNKS * CHUNK + key_off) >= 0
            s = jnp.where(valid, s, -jnp.inf)
        s_ref[j] = s
        m_ref[j] = jnp.broadcast_to(jnp.max(s, axis=1, keepdims=True), (2 * group, LANES))

    def finish(g, j, s_ref, m_ref):
        m = m_ref[j]
        e = jnp.exp2(s_ref[j] - jnp.concatenate([m] * (BAND2 // LANES), axis=1)).astype(BF16)
        v_old = jnp.concatenate([vp_ref[old(g), lanes(j)], ones_tile[old(g)]], axis=1)
        v_new = jnp.concatenate([vc_ref[new(g), lanes(j)], ones_tile[new(g)]], axis=1)
        r = (jnp.dot(e[:, 0:n_old(g)], v_old, preferred_element_type=F32)
             + jnp.dot(e[:, n_old(g):BAND2], v_new, preferred_element_type=F32))
        r = r[:, 0:2 * A_HD] * (1.0 / r[:, 2 * A_HD:4 * A_HD])
        o4 = jnp.concatenate(
            [jnp.where(first_head, r[2 * c * CHUNK:(2 * c + 1) * CHUNK, :],
                       r[(2 * c + 1) * CHUNK:(2 * c + 2) * CHUNK, :]) for c in range(2)], axis=0)
        o_s[rows(g), lanes(j)] = (o4 * sz_ref[rows(g), lanes(j)].astype(F32)).astype(BF16)

    even, odd = (s_even, m_even), (s_odd, m_odd)

    def run(masked):
        for j in range(A_PAIRS):
            scores(masked, 0, j, *even)
        for g in range(n_groups):
            cur, nxt = (even, odd) if g % 2 == 0 else (odd, even)
            for j in range(A_PAIRS):
                finish(g, j, *cur)
                if g + 1 < n_groups:
                    scores(masked, g + 1, j, *nxt)

    @pl.when(i == 0)
    def _():
        run(True)

    @pl.when(i > 0)
    def _():
        run(False)

    out_ref[...] = x_ref[...] + jnp.dot(o_s[...], wout_ref[...], preferred_element_type=F32)


def _attn(q, k, v, sz, x, bias, w_out):
    B, T, D = x.shape
    bm = ATT_BM
    nt = T // bm
    once = lambda *shape: pl.BlockSpec(shape, lambda b, t: (0,) * len(shape),
                                       pipeline_mode=pl.Buffered(1))
    cur = lambda width: pl.BlockSpec((None, bm, width), lambda b, t: (b, t, 0))
    prev = lambda width: pl.BlockSpec((None, bm, width), lambda b, t: (b, jnp.maximum(t - 1, 0), 0))
    return pl.pallas_call(
        _attn_kernel,
        grid=(B, nt),
        in_specs=[cur(A_INNER), prev(A_INNER), cur(A_INNER), prev(A_INNER), cur(A_INNER),
                  cur(A_INNER), cur(D), once(A_PAIRS, 4 * CHUNK, BAND2), once(A_INNER, D)],
        out_specs=cur(D),
        out_shape=jax.ShapeDtypeStruct((B, T, D), F32),
        scratch_shapes=[
            pltpu.VMEM((bm, A_INNER), BF16),
            pltpu.VMEM((A_PAIRS, 4 * CHUNK, BAND2), F32),
            pltpu.VMEM((A_PAIRS, 4 * CHUNK, BAND2), F32),
            pltpu.VMEM((A_PAIRS, 4 * CHUNK, LANES), F32),
            pltpu.VMEM((A_PAIRS, 4 * CHUNK, LANES), F32),
        ],
        compiler_params=pltpu.CompilerParams(
            dimension_semantics=("parallel", "parallel"), vmem_limit_bytes=VMEM_LIMIT),
        name="attn",
    )(q, k, k, v, v, sz, x, bias, w_out)


def kernel(x, norm_g, a_w_in, a_conv_w, a_conv_b, a_w_q, a_w_k, a_w_v, a_w_o, a_w_if, a_b_if,
           a_out_g, a_skip, a_w_out, b_w_in, b_q_g, b_k_g, b_rel_bias, b_w_out):
    assert x.shape[1] % ATT_BM == 0 and x.shape[2] == D_MODEL
    bf = lambda w: w.astype(BF16)
    pad_gate = lambda w: jnp.pad(w, ((0, 0), (0, GATE_PAD - 2 * M_HEADS)))

    x1 = _mlstm_layer(
        x, norm_g[0][None, :], bf(a_w_in[0]), a_conv_w[0], a_conv_b[0][None, :],
        bf(a_w_q[0]), bf(a_w_k[0]), bf(a_w_v[0]), bf(a_w_o[0]),
        bf(pad_gate(a_w_if[0])), pad_gate(a_b_if[0][None, :]), a_skip[0][None, :],
        a_out_g[0].reshape(1, M_INNER), bf(a_w_out[0]))

    bias = _attn_bias(b_rel_bias[0])
    gain = jnp.concatenate([jnp.tile(b_q_g[0], A_HEADS) * (A_HD ** -0.5 * LOG2_E),
                            jnp.tile(b_k_g[0], A_HEADS)])[None, :]
    q, k, v, sz = _attn_pre(x1, norm_g[1][None, :], bf(b_w_in[0]), gain)
    return _attn(q, k, v, sz, x1, bias, bf(b_w_out[0]))
```

```python
import functools

import jax
import jax.numpy as jnp
from jax import lax
from jax.experimental import pallas as pl
from jax.experimental.pallas import tpu as pltpu

F32 = jnp.float32
BF16 = jnp.bfloat16

RMS_EPS = 1e-6
CHUNK = 64
LOG2_E = 1.4426950408889634

D_MODEL = 1024
M_INNER = 2 * D_MODEL
M_HEADS = 4
M_DV = M_INNER // M_HEADS
M_DQK = M_DV // 2
CONV_K = 4

A_INNER = D_MODEL
A_HEADS = 16
A_HD = A_INNER // A_HEADS
A_PAIRS = A_HEADS // 2
LEFT_CHUNKS = 8
BAND = (LEFT_CHUNKS + 1) * CHUNK
REL_PAST = 256
REL_FUTURE = CHUNK - 1
REL_SIZE = REL_PAST + REL_FUTURE + 1

LANES = 128
SUBLANES = 8
GATE_PAD = LANES
VMEM_LIMIT = 56 * 1024 * 1024
FUSED_VMEM_LIMIT = 60 * 1024 * 1024
REC_L = 256
APRE_BM = 512
ATT_BM = LEFT_CHUNKS * CHUNK
BAND2 = BAND + CHUNK
BIAS_W = BAND2

NT_DIMS = (((1,), (1,)), ((), ()))
TN_DIMS = (((0,), (0,)), ((), ()))


def _sigmoid(x):
    return 0.5 * jnp.tanh(0.5 * x) + 0.5


def _log_sigmoid(x):
    return jnp.minimum(x, 0.0) - jnp.log1p(jnp.exp(-jnp.abs(x)))


def _rms(x, eps=RMS_EPS):
    return x * lax.rsqrt(jnp.mean(x * x, axis=-1, keepdims=True) + eps)


def _mlstm_layer_kernel(xp_ref, xc_ref, g_ref, win_ref, cw_ref, cb_ref, wq_ref, wk_ref, wv_ref, wo_ref,
                        wif_ref, bif_ref, skip_ref, og_ref, wout_ref,
                        out_ref,
                        ubuf, cb_scr, hq, hk, hv, ho, hcs, hsz, hgc, hgr,
                        ctx_ref, ctxb_ref, n_ref, m_ref, s_scr, qc_scr, qn_scr, rs_scr, p_scr, kw_scr,
                        pv_scr, cu_scr, y_scr,
                        *, blocks_per_row, n_blocks):
    step = pl.program_id(0)
    L = xp_ref.shape[0]
    wr = step % 2
    rd = 1 - wr
    prod_t = jnp.minimum(step, n_blocks - 1) % blocks_per_row
    cons_t = jnp.maximum(step - 1, 0) % blocks_per_row
    heads = range(M_HEADS)
    dv = lambda h: slice(h * M_DV, (h + 1) * M_DV)
    qk = lambda h: slice(h * M_DQK, (h + 1) * M_DQK)

    @pl.when(step == 0)
    def _():
        for ref in (hq, hk, hv, ho, hcs, hsz, hgc, hgr):
            ref[1] = jnp.zeros(ref.shape[1:], ref.dtype)

    @pl.when(prod_t == 0)
    def _():
        ubuf[0:SUBLANES, :] = jnp.zeros((SUBLANES, M_INNER), F32)

    @pl.when(cons_t == 0)
    def _():
        ctx_ref[...] = jnp.zeros(ctx_ref.shape, F32)
        ctxb_ref[...] = jnp.zeros(ctxb_ref.shape, BF16)
        n_ref[...] = jnp.zeros(n_ref.shape, F32)
        m_ref[...] = jnp.zeros(m_ref.shape, F32)

    for h in heads:
        qh = hq[rd, :, qk(h)]
        s_scr[h] = lax.dot_general(qh, hk[rd, :, qk(h)], NT_DIMS, preferred_element_type=F32)
        qc_scr[h] = jnp.dot(qh, ctxb_ref[h], preferred_element_type=F32)
        qn_scr[h] = jnp.broadcast_to(
            jnp.sum(qh.astype(F32) * n_ref[h], axis=1, keepdims=True), (L, LANES))

    xn = (_rms(xp_ref[...]) * g_ref[...]).astype(BF16)

    gc = hgc[rd]
    b = gc
    ar = pltpu.roll(gc, GATE_PAD - M_HEADS, 1)
    cm = pltpu.roll(gc, GATE_PAD - 2 * M_HEADS, 1)
    m_prev = m_ref[...]
    g_last = b[L - 1:L, :]
    a_c = g_last + ar
    m_new = jnp.maximum(g_last + m_prev, jnp.max(a_c, axis=0, keepdims=True))
    w_all = jnp.exp(a_c - m_new)
    decay = jnp.exp(g_last + m_prev - m_new)
    mx = jnp.maximum(m_prev, cm)
    inter_all = jnp.exp(m_prev - mx)
    enmt_all = jnp.exp(-(b + mx))
    m_ref[...] = m_new
    gr = hgr[rd]
    tril = (lax.broadcasted_iota(jnp.int32, (L, L), 1)
            <= lax.broadcasted_iota(jnp.int32, (L, L), 0))

    for h in heads:
        d = jnp.where(tril, jnp.exp(gr[h:h + 1, :] - mx[:, h:h + 1]), 0.0)
        p = s_scr[h] * d
        p_scr[h] = p.astype(BF16)
        rs_scr[h] = jnp.broadcast_to(jnp.sum(p, axis=1, keepdims=True), (L, LANES))
        kw = hk[rd, :, qk(h)].astype(F32) * w_all[:, h:h + 1]
        kw_scr[h] = kw.astype(BF16)
        n_ref[h] = decay[:, h:h + 1] * n_ref[h] + jnp.sum(kw, axis=0, keepdims=True)

    for h in heads:
        ubuf[SUBLANES:SUBLANES + L, dv(h)] = jnp.dot(xn, win_ref[:, dv(h)],
                                                     preferred_element_type=F32)

    vx = lambda h: hv[rd, :, dv(h)]

    def numerators_and_increments():
        for h in heads:
            pv_scr[h] = jnp.dot(p_scr[h], vx(h), preferred_element_type=F32)
            cu_scr[h] = lax.dot_general(kw_scr[h], vx(h), TN_DIMS, preferred_element_type=F32)

    def conv_branch(h):
        u = ubuf[SUBLANES:SUBLANES + L, dv(h)]
        conv = cb_ref[:, dv(h)] + cw_ref[CONV_K - 1:CONV_K, dv(h)] * u
        for j in range(CONV_K - 1):
            shifted = ubuf[pl.ds(SUBLANES - (CONV_K - 1) + j, L), dv(h)]
            conv = conv + cw_ref[j:j + 1, dv(h)] * shifted
        c = conv * _sigmoid(conv)
        cb_scr[:, dv(h)] = c.astype(BF16)
        hcs[wr, :, dv(h)] = (skip_ref[:, dv(h)] * c).astype(BF16)

    def value_projections(h):
        ub16 = ubuf[SUBLANES:SUBLANES + L, dv(h)].astype(BF16)
        hv[wr, :, dv(h)] = jnp.dot(ub16, wv_ref[h], preferred_element_type=F32).astype(BF16)
        ho[wr, :, dv(h)] = _sigmoid(jnp.dot(ub16, wo_ref[h], preferred_element_type=F32)).astype(BF16)

    def key_projections(h):
        cb16 = cb_scr[:, dv(h)]
        hq[wr, :, qk(h)] = jnp.dot(cb16, wq_ref[h], preferred_element_type=F32).astype(BF16)
        hk[wr, :, qk(h)] = (jnp.dot(cb16, wk_ref[h], preferred_element_type=F32)
                            * (M_DQK ** -0.5)).astype(BF16)

    def gate_branch(h):
        z = jnp.dot(xn, win_ref[:, M_INNER + h * M_DV:M_INNER + (h + 1) * M_DV],
                    preferred_element_type=F32)
        hsz[wr, :, dv(h)] = (z * _sigmoid(z)).astype(BF16)

    def normalise(h):
        inter = jnp.broadcast_to(inter_all[:, h:h + 1], (L, LANES))
        enmt = jnp.broadcast_to(enmt_all[:, h:h + 1], (L, LANES))
        num = pv_scr[h] + jnp.tile(inter, (1, M_DV // LANES)) * qc_scr[h]
        den = rs_scr[h] + inter * qn_scr[h]
        r = 1.0 / jnp.maximum(jnp.abs(den), enmt)
        ms = jnp.mean(num * num, axis=1, keepdims=True)
        scale = r * lax.rsqrt(r * r * ms + RMS_EPS)
        hn = (num * jnp.tile(scale, (1, M_DV // LANES)) * og_ref[:, dv(h)]).astype(BF16)
        y_scr[:, dv(h)] = (ho[rd, :, dv(h)] * hn + hcs[rd, :, dv(h)]) * hsz[rd, :, dv(h)]

    def project(h):
        part = jnp.dot(y_scr[:, dv(h)], wout_ref[dv(h), :], preferred_element_type=F32)
        if h == 0:
            out_ref[...] = xc_ref[...] + part
        else:
            out_ref[...] += part

    for h in heads:
        conv_branch(h)
    ubuf[0:SUBLANES, :] = ubuf[L:L + SUBLANES, :]
    numerators_and_increments()

    def gate_algebra():
        gates = bif_ref[...] + jnp.dot(cb_scr[...], wif_ref[...], preferred_element_type=F32)
        lf = _log_sigmoid(gates)
        lf_hi = lf.astype(BF16)
        rem = lf - lf_hi.astype(F32)
        lf_mid = rem.astype(BF16)
        lf_lo = (rem - lf_mid.astype(F32)).astype(BF16)
        trilb = tril.astype(BF16)
        cum = (jnp.dot(trilb, lf_hi, preferred_element_type=F32)
               + jnp.dot(trilb, lf_mid, preferred_element_type=F32)
               + jnp.dot(trilb, lf_lo, preferred_element_type=F32))
        nb = pltpu.roll(cum, GATE_PAD - M_HEADS, 1)
        nar = gates - nb
        row = lax.broadcasted_iota(jnp.int32, (L, GATE_PAD), 0)
        ncm = nar
        shift = 1
        while shift < L:
            ncm = jnp.maximum(ncm, jnp.where(row >= shift, pltpu.roll(ncm, shift, 0), -jnp.inf))
            shift *= 2
        lane = lax.broadcasted_iota(jnp.int32, (L, GATE_PAD), 1)
        hgc[wr] = jnp.where(lane < M_HEADS, nb,
                            jnp.where(lane < 2 * M_HEADS, pltpu.roll(nar, M_HEADS, 1),
                                      pltpu.roll(ncm, 2 * M_HEADS, 1)))
        hgr[wr] = nar.T[0:2 * M_HEADS, :]

    for h in heads:
        if h == M_HEADS // 2:
            gate_algebra()
        gate_branch(h)
        value_projections(h)
        key_projections(h)
        normalise(h)
        if h > 0:
            project(h - 1)
    project(M_HEADS - 1)

    for h in heads:
        ctx = decay[:, h:h + 1] * ctx_ref[h] + cu_scr[h]
        ctx_ref[h] = ctx
        ctxb_ref[h] = ctx.astype(BF16)


def _mlstm_layer(x, g, w_in, conv_w, conv_b, w_q, w_k, w_v, w_o, w_if, b_if, skip, out_g, w_out):
    B, T, D = x.shape
    L = REC_L
    nt = T // L
    n_blocks = B * nt
    dvx = M_DV
    once = lambda *shape: pl.BlockSpec(shape, lambda s: (0,) * len(shape), pipeline_mode=pl.Buffered(1))

    def prod_map(s):
        blk = jnp.minimum(s, n_blocks - 1)
        return (blk // nt, blk % nt, 0)

    def cons_map(s):
        blk = jnp.maximum(s - 1, 0)
        return (blk // nt, blk % nt, 0)

    hand = lambda width, dtype: pltpu.VMEM((2, L, width), dtype)
    return pl.pallas_call(
        functools.partial(_mlstm_layer_kernel, blocks_per_row=nt, n_blocks=n_blocks),
        grid=(n_blocks + 1,),
        in_specs=[
            pl.BlockSpec((None, L, D), prod_map), pl.BlockSpec((None, L, D), cons_map),
            once(1, D), once(D, 2 * M_INNER), once(CONV_K, M_INNER), once(1, M_INNER),
            once(M_HEADS, M_DV, M_DQK), once(M_HEADS, M_DV, M_DQK),
            once(M_HEADS, M_DV, M_DV), once(M_HEADS, M_DV, M_DV),
            once(M_INNER, GATE_PAD), once(1, GATE_PAD), once(1, M_INNER),
            once(1, M_INNER), once(M_INNER, D),
        ],
        out_specs=pl.BlockSpec((None, L, D), cons_map),
        out_shape=jax.ShapeDtypeStruct((B, T, D), F32),
        scratch_shapes=[
            pltpu.VMEM((SUBLANES + L, M_INNER), F32),
            pltpu.VMEM((L, M_INNER), BF16),
            hand(M_HEADS * M_DQK, BF16), hand(M_HEADS * M_DQK, BF16),
            hand(M_INNER, BF16), hand(M_INNER, BF16),
            hand(M_INNER, BF16), hand(M_INNER, BF16),
            hand(GATE_PAD, F32), pltpu.VMEM((2, 2 * M_HEADS, L), F32),
            pltpu.VMEM((M_HEADS, M_DQK, dvx), F32),
            pltpu.VMEM((M_HEADS, M_DQK, dvx), BF16),
            pltpu.VMEM((M_HEADS, 1, M_DQK), F32),
            pltpu.VMEM((1, LANES), F32),
            pltpu.VMEM((M_HEADS, L, L), F32),
            pltpu.VMEM((M_HEADS, L, dvx), F32),
            pltpu.VMEM((M_HEADS, L, LANES), F32),
            pltpu.VMEM((M_HEADS, L, LANES), F32),
            pltpu.VMEM((M_HEADS, L, L), BF16),
            pltpu.VMEM((M_HEADS, L, M_DQK), BF16),
            pltpu.VMEM((M_HEADS, L, dvx), F32),
            pltpu.VMEM((M_HEADS, M_DQK, dvx), F32),
            pltpu.VMEM((L, M_INNER), BF16),
        ],
        compiler_params=pltpu.CompilerParams(
            dimension_semantics=("arbitrary",), vmem_limit_bytes=FUSED_VMEM_LIMIT),
        name="mlstm_layer",
    )(x, x, g, w_in, conv_w, conv_b, w_q, w_k, w_v, w_o, w_if, b_if, skip, out_g, w_out)


def _bias_kernel(tab_ref, out_ref):
    r = lax.broadcasted_iota(jnp.int32, (REL_SIZE, BIAS_W), 0)
    e = lax.broadcasted_iota(jnp.int32, (REL_SIZE, BIAS_W), 1)
    e = jnp.where(e >= BAND, e - BIAS_W, e)
    idx = jnp.clip(LEFT_CHUNKS * CHUNK - e, -REL_FUTURE, REL_PAST) + REL_FUTURE
    onehot = (r == idx).astype(F32)
    g = jnp.dot(tab_ref[...], onehot, precision=lax.Precision.HIGHEST,
                preferred_element_type=F32) * LOG2_E
    for j in range(A_PAIRS):
        halves = []
        for a in range(2):
            rows = jnp.broadcast_to(g[2 * j + a:2 * j + a + 1, :], (CHUNK, BIAS_W))
            halves.append(pltpu.roll(rows, 0, 1, stride=1, stride_axis=0))
        both = jnp.concatenate(halves, axis=0)
        col = lax.broadcasted_iota(jnp.int32, (2 * CHUNK, BAND2), 1)
        first = jnp.where(col < BAND, both, -jnp.inf)
        second = jnp.where(col >= CHUNK, pltpu.roll(both, CHUNK, 1), -jnp.inf)
        out_ref[j] = jnp.concatenate([first, second], axis=0)


def _attn_bias(rel_bias):
    return pl.pallas_call(
        _bias_kernel,
        out_shape=jax.ShapeDtypeStruct((A_PAIRS, 4 * CHUNK, BAND2), F32),
        compiler_params=pltpu.CompilerParams(vmem_limit_bytes=VMEM_LIMIT),
        name="attn_bias",
    )(rel_bias)


def _attn_pre_kernel(x_ref, g_ref, win_ref, gain_ref,
                     q_ref, k_ref, v_ref, sz_ref,
                     qk_scr):
    bm = x_ref.shape[0]
    xn = (_rms(x_ref[...]) * g_ref[...]).astype(BF16)
    qk_scr[...] = jnp.dot(xn, win_ref[:, 0:2 * A_INNER], preferred_element_type=F32)
    v_ref[...] = jnp.dot(xn, win_ref[:, 2 * A_INNER:3 * A_INNER],
                         preferred_element_type=F32).astype(BF16)
    z = jnp.dot(xn, win_ref[:, 3 * A_INNER:4 * A_INNER], preferred_element_type=F32)
    sz_ref[...] = (z * _sigmoid(z)).astype(BF16)

    first_head = lax.broadcasted_iota(jnp.int32, (bm, 2 * A_HD), 1) < A_HD
    for j in range(2 * A_PAIRS):
        lanes = slice(j * 2 * A_HD, (j + 1) * 2 * A_HD)
        xt = qk_scr[:, lanes]
        sq = xt * xt
        head_a = jnp.sum(jnp.where(first_head, sq, 0.0), axis=1, keepdims=True)
        head_b = jnp.sum(jnp.where(first_head, 0.0, sq), axis=1, keepdims=True)
        inv_a = lax.rsqrt(head_a * (1.0 / A_HD) + RMS_EPS)
        inv_b = lax.rsqrt(head_b * (1.0 / A_HD) + RMS_EPS)
        out = (xt * jnp.where(first_head, inv_a, inv_b) * gain_ref[:, lanes]).astype(BF16)
        if j < A_PAIRS:
            q_ref[:, lanes] = out
        else:
            k_ref[:, j * 2 * A_HD - A_INNER:(j + 1) * 2 * A_HD - A_INNER] = out


def _attn_pre(x, g, w_in, gain):
    B, T, D = x.shape
    bm = min(APRE_BM, T)
    nt = T // bm
    full = lambda *shape: pl.BlockSpec(shape, lambda b, t: (0,) * len(shape))
    tok = lambda width: pl.BlockSpec((None, bm, width), lambda b, t: (b, t, 0))
    o = jax.ShapeDtypeStruct((B, T, A_INNER), BF16)
    return pl.pallas_call(
        _attn_pre_kernel,
        grid=(B, nt),
        in_specs=[tok(D), full(1, D), full(D, 4 * A_INNER), full(1, 2 * A_INNER)],
        out_specs=(tok(A_INNER),) * 4,
        out_shape=(o, o, o, o),
        scratch_shapes=[pltpu.VMEM((bm, 2 * A_INNER), F32)],
        compiler_params=pltpu.CompilerParams(
            dimension_semantics=("parallel", "parallel"), vmem_limit_bytes=VMEM_LIMIT),
        name="attn_pre",
    )(x, g, w_in, gain)


def _attn_kernel(q_ref, kp_ref, kc_ref, vp_ref, vc_ref, sz_ref, x_ref, bias_ref, wout_ref,
                 out_ref,
                 o_s, s_even, s_odd, m_even, m_odd):
    i = pl.program_id(1)
    bm = q_ref.shape[0]
    group = 2 * CHUNK
    n_groups = bm // group
    first_head = lax.broadcasted_iota(jnp.int32, (CHUNK, 2 * A_HD), 1) < A_HD
    key_off = lax.broadcasted_iota(jnp.int32, (1, BAND2), 1)
    ones_tile = jnp.ones((bm, LANES), BF16)

    lanes = lambda j: slice(j * 2 * A_HD, (j + 1) * 2 * A_HD)
    rows = lambda g: slice(g * group, (g + 1) * group)
    old = lambda g: slice(g * group + bm - LEFT_CHUNKS * CHUNK, bm)
    new = lambda g: slice(0, (g + 1) * group)
    n_old = lambda g: LEFT_CHUNKS * CHUNK - g * group

    def scores(masked, g, j, s_ref, m_ref):
        q4 = q_ref[rows(g), lanes(j)].astype(F32)
        parts = []
        for c in range(2):
            qc = q4[c * CHUNK:(c + 1) * CHUNK, :]
            parts += [jnp.where(first_head, qc, 0.0), jnp.where(first_head, 0.0, qc)]
        wt = jnp.concatenate(parts, axis=0).astype(BF16)
        s = jnp.concatenate(
            [lax.dot_general(wt, kp_ref[old(g), lanes(j)], NT_DIMS, preferred_element_type=F32),
             lax.dot_general(wt, kc_ref[new(g), lanes(j)], NT_DIMS, preferred_element_type=F32)],
            axis=1) + bias_ref[j]
        if masked:
            valid = (g * group - LEFT_CHUNKS * CHUNK + key_off) >= 0
            s = jnp.where(valid, s, -jnp.inf)
        s_ref[j] = s
        m_ref[j] = jnp.broadcast_to(jnp.max(s, axis=1, keepdims=True), (2 * group, LANES))

    def finish(g, j, s_ref, m_ref):
        m = m_ref[j]
        e = jnp.exp2(s_ref[j] - jnp.concatenate([m] * (BAND2 // LANES), axis=1)).astype(BF16)
        v_old = jnp.concatenate([vp_ref[old(g), lanes(j)], ones_tile[old(g)]], axis=1)
        v_new = jnp.concatenate([vc_ref[new(g), lanes(j)], ones_tile[new(g)]], axis=1)
        r = (jnp.dot(e[:, 0:n_old(g)], v_old, preferred_element_type=F32)
             + jnp.dot(e[:, n_old(g):BAND2], v_new, preferred_element_type=F32))
        r = r[:, 0:2 * A_HD] * (1.0 / r[:, 2 * A_HD:4 * A_HD])
        o4 = jnp.concatenate(
            [jnp.where(first_head, r[2 * c * CHUNK:(2 * c + 1) * CHUNK, :],
                       r[(2 * c + 1) * CHUNK:(2 * c + 2) * CHUNK, :]) for c in range(2)], axis=0)
        o_s[rows(g), lanes(j)] = (o4 * sz_ref[rows(g), lanes(j)].astype(F32)).astype(BF16)

    even, odd = (s_even, m_even), (s_odd, m_odd)

    def run(masked):
        for j in range(A_PAIRS):
            scores(masked, 0, j, *even)
        for g in range(n_groups):
            cur, nxt = (even, odd) if g % 2 == 0 else (odd, even)
            for j in range(A_PAIRS):
                finish(g, j, *cur)
                if g + 1 < n_groups:
                    scores(masked, g + 1, j, *nxt)

    @pl.when(i == 0)
    def _():
        run(True)

    @pl.when(i > 0)
    def _():
        run(False)

    out_ref[...] = x_ref[...] + jnp.dot(o_s[...], wout_ref[...], preferred_element_type=F32)


def _attn(q, k, v, sz, x, bias, w_out):
    B, T, D = x.shape
    bm = ATT_BM
    nt = T // bm
    once = lambda *shape: pl.BlockSpec(shape, lambda b, t: (0,) * len(shape),
                                       pipeline_mode=pl.Buffered(1))
    cur = lambda width: pl.BlockSpec((None, bm, width), lambda b, t: (b, t, 0))
    prev = lambda width: pl.BlockSpec((None, bm, width), lambda b, t: (b, jnp.maximum(t - 1, 0), 0))
    return pl.pallas_call(
        _attn_kernel,
        grid=(B, nt),
        in_specs=[cur(A_INNER), prev(A_INNER), cur(A_INNER), prev(A_INNER), cur(A_INNER),
                  cur(A_INNER), cur(D), once(A_PAIRS, 4 * CHUNK, BAND2), once(A_INNER, D)],
        out_specs=cur(D),
        out_shape=jax.ShapeDtypeStruct((B, T, D), F32),
        scratch_shapes=[
            pltpu.VMEM((bm, A_INNER), BF16),
            pltpu.VMEM((A_PAIRS, 4 * CHUNK, BAND2), F32),
            pltpu.VMEM((A_PAIRS, 4 * CHUNK, BAND2), F32),
            pltpu.VMEM((A_PAIRS, 4 * CHUNK, LANES), F32),
            pltpu.VMEM((A_PAIRS, 4 * CHUNK, LANES), F32),
        ],
        compiler_params=pltpu.CompilerParams(
            dimension_semantics=("parallel", "parallel"), vmem_limit_bytes=VMEM_LIMIT),
        name="attn",
    )(q, k, k, v, v, sz, x, bias, w_out)


def kernel(x, norm_g, a_w_in, a_conv_w, a_conv_b, a_w_q, a_w_k, a_w_v, a_w_o, a_w_if, a_b_if,
           a_out_g, a_skip, a_w_out, b_w_in, b_q_g, b_k_g, b_rel_bias, b_w_out):
    assert x.shape[1] % ATT_BM == 0 and x.shape[2] == D_MODEL
    bf = lambda w: w.astype(BF16)
    pad_gate = lambda w: jnp.pad(w, ((0, 0), (0, GATE_PAD - 2 * M_HEADS)))

    x1 = _mlstm_layer(
        x, norm_g[0][None, :], bf(a_w_in[0]), a_conv_w[0], a_conv_b[0][None, :],
        bf(a_w_q[0]), bf(a_w_k[0]), bf(a_w_v[0]), bf(a_w_o[0]),
        bf(pad_gate(a_w_if[0])), pad_gate(a_b_if[0][None, :]), a_skip[0][None, :],
        a_out_g[0].reshape(1, M_INNER), bf(a_w_out[0]))

    bias = _attn_bias(b_rel_bias[0])
    gain = jnp.concatenate([jnp.tile(b_q_g[0], A_HEADS) * (A_HD ** -0.5 * LOG2_E),
                            jnp.tile(b_k_g[0], A_HEADS)])[None, :]
    q, k, v, sz = _attn_pre(x1, norm_g[1][None, :], bf(b_w_in[0]), gain)
    return _attn(q, k, v, sz, x1, bias, bf(b_w_out[0]))
```

```python
import functools

import jax
import jax.numpy as jnp
from jax import lax
from jax.experimental import pallas as pl
from jax.experimental.pallas import tpu as pltpu

F32 = jnp.float32
BF16 = jnp.bfloat16

RMS_EPS = 1e-6
CHUNK = 64
LOG2_E = 1.4426950408889634

D_MODEL = 1024
M_INNER = 2 * D_MODEL
M_HEADS = 4
M_DV = M_INNER // M_HEADS
M_DQK = M_DV // 2
CONV_K = 4

A_INNER = D_MODEL
A_HEADS = 16
A_HD = A_INNER // A_HEADS
A_PAIRS = A_HEADS // 2
LEFT_CHUNKS = 8
BAND = (LEFT_CHUNKS + 1) * CHUNK
REL_PAST = 256
REL_FUTURE = CHUNK - 1
REL_SIZE = REL_PAST + REL_FUTURE + 1

LANES = 128
SUBLANES = 8
GATE_PAD = LANES
VMEM_LIMIT = 56 * 1024 * 1024
FUSED_VMEM_LIMIT = 60 * 1024 * 1024
REC_L = 256
APRE_BM = 512
ATT_BM = LEFT_CHUNKS * CHUNK
BAND2 = BAND + CHUNK
BIAS_W = BAND2

NT_DIMS = (((1,), (1,)), ((), ()))
TN_DIMS = (((0,), (0,)), ((), ()))


def _sigmoid(x):
    return 0.5 * jnp.tanh(0.5 * x) + 0.5


def _log_sigmoid(x):
    return jnp.minimum(x, 0.0) - jnp.log1p(jnp.exp(-jnp.abs(x)))


def _rms(x, eps=RMS_EPS):
    return x * lax.rsqrt(jnp.mean(x * x, axis=-1, keepdims=True) + eps)


def _mlstm_layer_kernel(xp_ref, xc_ref, g_ref, win_ref, cw_ref, cb_ref, wq_ref, wk_ref, wv_ref, wo_ref,
                        wif_ref, bif_ref, skip_ref, og_ref, wout_ref,
                        out_ref,
                        ubuf, cb_scr, hq, hk, hv, ho, hcs, hsz, hgc, hgr,
                        ctx_ref, ctxb_ref, n_ref, m_ref, s_scr, qc_scr, qn_scr, rs_scr, p_scr, kw_scr,
                        pv_scr, cu_scr, y_scr,
                        *, blocks_per_row, n_blocks):
    step = pl.program_id(0)
    L = xp_ref.shape[0]
    wr = step % 2
    rd = 1 - wr
    prod_t = jnp.minimum(step, n_blocks - 1) % blocks_per_row
    cons_t = jnp.maximum(step - 1, 0) % blocks_per_row
    heads = range(M_HEADS)
    dv = lambda h: slice(h * M_DV, (h + 1) * M_DV)
    qk = lambda h: slice(h * M_DQK, (h + 1) * M_DQK)

    @pl.when(step == 0)
    def _():
        for ref in (hq, hk, hv, ho, hcs, hsz, hgc, hgr):
            ref[1] = jnp.zeros(ref.shape[1:], ref.dtype)

    @pl.when(prod_t == 0)
    def _():
        ubuf[0:SUBLANES, :] = jnp.zeros((SUBLANES, M_INNER), F32)

    @pl.when(cons_t == 0)
    def _():
        ctx_ref[...] = jnp.zeros(ctx_ref.shape, F32)
        ctxb_ref[...] = jnp.zeros(ctxb_ref.shape, BF16)
        n_ref[...] = jnp.zeros(n_ref.shape, F32)
        m_ref[...] = jnp.zeros(m_ref.shape, F32)

    for h in heads:
        qh = hq[rd, :, qk(h)]
        s_scr[h] = lax.dot_general(qh, hk[rd, :, qk(h)], NT_DIMS, preferred_element_type=F32)
        qc_scr[h] = jnp.dot(qh, ctxb_ref[h], preferred_element_type=F32)
        qn_scr[h] = jnp.broadcast_to(
            jnp.sum(qh.astype(F32) * n_ref[h], axis=1, keepdims=True), (L, LANES))

    xn = (_rms(xp_ref[...]) * g_ref[...]).astype(BF16)

    gc = hgc[rd]
    b = gc
    ar = pltpu.roll(gc, GATE_PAD - M_HEADS, 1)
    cm = pltpu.roll(gc, GATE_PAD - 2 * M_HEADS, 1)
    m_prev = m_ref[...]
    g_last = b[L - 1:L, :]
    a_c = g_last + ar
    m_new = jnp.maximum(g_last + m_prev, jnp.max(a_c, axis=0, keepdims=True))
    w_all = jnp.exp(a_c - m_new)
    decay = jnp.exp(g_last + m_prev - m_new)
    mx = jnp.maximum(m_prev, cm)
    inter_all = jnp.exp(m_prev - mx)
    enmt_all = jnp.exp(-(b + mx))
    m_ref[...] = m_new
    gr = hgr[rd]
    tril = (lax.broadcasted_iota(jnp.int32, (L, L), 1)
            <= lax.broadcasted_iota(jnp.int32, (L, L), 0))

    for h in heads:
        d = jnp.where(tril, jnp.exp(gr[h:h + 1, :] - mx[:, h:h + 1]), 0.0)
        p = s_scr[h] * d
        p_scr[h] = p.astype(BF16)
        rs_scr[h] = jnp.broadcast_to(jnp.sum(p, axis=1, keepdims=True), (L, LANES))
        kw = hk[rd, :, qk(h)].astype(F32) * w_all[:, h:h + 1]
        kw_scr[h] = kw.astype(BF16)
        n_ref[h] = decay[:, h:h + 1] * n_ref[h] + jnp.sum(kw, axis=0, keepdims=True)

    for h in heads:
        ubuf[SUBLANES:SUBLANES + L, dv(h)] = jnp.dot(xn, win_ref[:, dv(h)],
                                                     preferred_element_type=F32)

    def numerators_and_increments():
        for h in heads:
            vh = hv[rd, :, dv(h)]
            pv_scr[h] = jnp.dot(p_scr[h], vh, preferred_element_type=F32)
            cu_scr[h] = lax.dot_general(kw_scr[h], vh, TN_DIMS, preferred_element_type=F32)

    def conv_branch(h):
        u = ubuf[SUBLANES:SUBLANES + L, dv(h)]
        conv = cb_ref[:, dv(h)] + cw_ref[CONV_K - 1:CONV_K, dv(h)] * u
        for j in range(CONV_K - 1):
            shifted = ubuf[pl.ds(SUBLANES - (CONV_K - 1) + j, L), dv(h)]
            conv = conv + cw_ref[j:j + 1, dv(h)] * shifted
        c = conv * _sigmoid(conv)
        cb_scr[:, dv(h)] = c.astype(BF16)
        hcs[wr, :, dv(h)] = (skip_ref[:, dv(h)] * c).astype(BF16)

    def value_projections(h):
        ub16 = ubuf[SUBLANES:SUBLANES + L, dv(h)].astype(BF16)
        hv[wr, :, dv(h)] = jnp.dot(ub16, wv_ref[h], preferred_element_type=F32).astype(BF16)
        ho[wr, :, dv(h)] = _sigmoid(jnp.dot(ub16, wo_ref[h], preferred_element_type=F32)).astype(BF16)

    def key_projections(h):
        cb16 = cb_scr[:, dv(h)]
        hq[wr, :, qk(h)] = jnp.dot(cb16, wq_ref[h], preferred_element_type=F32).astype(BF16)
        hk[wr, :, qk(h)] = (jnp.dot(cb16, wk_ref[h], preferred_element_type=F32)
                            * (M_DQK ** -0.5)).astype(BF16)

    def gate_branch(h):
        z = jnp.dot(xn, win_ref[:, M_INNER + h * M_DV:M_INNER + (h + 1) * M_DV],
                    preferred_element_type=F32)
        hsz[wr, :, dv(h)] = (z * _sigmoid(z)).astype(BF16)

    def normalise(h):
        inter = jnp.broadcast_to(inter_all[:, h:h + 1], (L, LANES))
        enmt = jnp.broadcast_to(enmt_all[:, h:h + 1], (L, LANES))
        num = pv_scr[h] + jnp.tile(inter, (1, M_DV // LANES)) * qc_scr[h]
        den = rs_scr[h] + inter * qn_scr[h]
        r = 1.0 / jnp.maximum(jnp.abs(den), enmt)
        ms = jnp.mean(num * num, axis=1, keepdims=True)
        scale = r * lax.rsqrt(r * r * ms + RMS_EPS)
        hn = (num * jnp.tile(scale, (1, M_DV // LANES)) * og_ref[:, dv(h)]).astype(BF16)
        y_scr[:, dv(h)] = (ho[rd, :, dv(h)] * hn + hcs[rd, :, dv(h)]) * hsz[rd, :, dv(h)]

    def project(h):
        part = jnp.dot(y_scr[:, dv(h)], wout_ref[dv(h), :], preferred_element_type=F32)
        if h == 0:
            out_ref[...] = xc_ref[...] + part
        else:
            out_ref[...] += part

    for h in heads:
        conv_branch(h)
    ubuf[0:SUBLANES, :] = ubuf[L:L + SUBLANES, :]
    numerators_and_increments()

    def gate_algebra():
        gates = bif_ref[...] + jnp.dot(cb_scr[...], wif_ref[...], preferred_element_type=F32)
        lf = _log_sigmoid(gates)
        lf_hi = lf.astype(BF16)
        rem = lf - lf_hi.astype(F32)
        lf_mid = rem.astype(BF16)
        lf_lo = (rem - lf_mid.astype(F32)).astype(BF16)
        trilb = tril.astype(BF16)
        cum = (jnp.dot(trilb, lf_hi, preferred_element_type=F32)
               + jnp.dot(trilb, lf_mid, preferred_element_type=F32)
               + jnp.dot(trilb, lf_lo, preferred_element_type=F32))
        nb = pltpu.roll(cum, GATE_PAD - M_HEADS, 1)
        nar = gates - nb
        row = lax.broadcasted_iota(jnp.int32, (L, GATE_PAD), 0)
        ncm = nar
        shift = 1
        while shift < L:
            ncm = jnp.maximum(ncm, jnp.where(row >= shift, pltpu.roll(ncm, shift, 0), -jnp.inf))
            shift *= 2
        lane = lax.broadcasted_iota(jnp.int32, (L, GATE_PAD), 1)
        hgc[wr] = jnp.where(lane < M_HEADS, nb,
                            jnp.where(lane < 2 * M_HEADS, pltpu.roll(nar, M_HEADS, 1),
                                      pltpu.roll(ncm, 2 * M_HEADS, 1)))
        hgr[wr] = nar.T[0:2 * M_HEADS, :]

    for h in heads:
        if h == M_HEADS // 2:
            gate_algebra()
        gate_branch(h)
        value_projections(h)
        key_projections(h)
        normalise(h)
        if h > 0:
            project(h - 1)
    project(M_HEADS - 1)

    for h in heads:
        ctx = decay[:, h:h + 1] * ctx_ref[h] + cu_scr[h]
        ctx_ref[h] = ctx
        ctxb_ref[h] = ctx.astype(BF16)


def _mlstm_layer(x, g, w_in, conv_w, conv_b, w_q, w_k, w_v, w_o, w_if, b_if, skip, out_g, w_out):
    B, T, D = x.shape
    L = REC_L
    nt = T // L
    n_blocks = B * nt
    once = lambda *shape: pl.BlockSpec(shape, lambda s: (0,) * len(shape), pipeline_mode=pl.Buffered(1))

    def prod_map(s):
        blk = jnp.minimum(s, n_blocks - 1)
        return (blk // nt, blk % nt, 0)

    def cons_map(s):
        blk = jnp.maximum(s - 1, 0)
        return (blk // nt, blk % nt, 0)

    hand = lambda width, dtype: pltpu.VMEM((2, L, width), dtype)
    return pl.pallas_call(
        functools.partial(_mlstm_layer_kernel, blocks_per_row=nt, n_blocks=n_blocks),
        grid=(n_blocks + 1,),
        in_specs=[
            pl.BlockSpec((None, L, D), prod_map), pl.BlockSpec((None, L, D), cons_map),
            once(1, D), once(D, 2 * M_INNER), once(CONV_K, M_INNER), once(1, M_INNER),
            once(M_HEADS, M_DV, M_DQK), once(M_HEADS, M_DV, M_DQK),
            once(M_HEADS, M_DV, M_DV), once(M_HEADS, M_DV, M_DV),
            once(M_INNER, GATE_PAD), once(1, GATE_PAD), once(1, M_INNER),
            once(1, M_INNER), once(M_INNER, D),
        ],
        out_specs=pl.BlockSpec((None, L, D), cons_map),
        out_shape=jax.ShapeDtypeStruct((B, T, D), F32),
        scratch_shapes=[
            pltpu.VMEM((SUBLANES + L, M_INNER), F32),
            pltpu.VMEM((L, M_INNER), BF16),
            hand(M_HEADS * M_DQK, BF16), hand(M_HEADS * M_DQK, BF16),
            hand(M_INNER, BF16), hand(M_INNER, BF16),
            hand(M_INNER, BF16), hand(M_INNER, BF16),
            hand(GATE_PAD, F32), pltpu.VMEM((2, 2 * M_HEADS, L), F32),
            pltpu.VMEM((M_HEADS, M_DQK, M_DV), F32),
            pltpu.VMEM((M_HEADS, M_DQK, M_DV), BF16),
            pltpu.VMEM((M_HEADS, 1, M_DQK), F32),
            pltpu.VMEM((1, LANES), F32),
            pltpu.VMEM((M_HEADS, L, L), F32),
            pltpu.VMEM((M_HEADS, L, M_DV), F32),
            pltpu.VMEM((M_HEADS, L, LANES), F32),
            pltpu.VMEM((M_HEADS, L, LANES), F32),
            pltpu.VMEM((M_HEADS, L, L), BF16),
            pltpu.VMEM((M_HEADS, L, M_DQK), BF16),
            pltpu.VMEM((M_HEADS, L, M_DV), F32),
            pltpu.VMEM((M_HEADS, M_DQK, M_DV), F32),
            pltpu.VMEM((L, M_INNER), BF16),
        ],
        compiler_params=pltpu.CompilerParams(
            dimension_semantics=("arbitrary",), vmem_limit_bytes=FUSED_VMEM_LIMIT),
        name="mlstm_layer",
    )(x, x, g, w_in, conv_w, conv_b, w_q, w_k, w_v, w_o, w_if, b_if, skip, out_g, w_out)


def _bias_kernel(tab_ref, out_ref):
    r = lax.broadcasted_iota(jnp.int32, (REL_SIZE, BIAS_W), 0)
    e = lax.broadcasted_iota(jnp.int32, (REL_SIZE, BIAS_W), 1)
    e = jnp.where(e >= BAND, e - BIAS_W, e)
    idx = jnp.clip(LEFT_CHUNKS * CHUNK - e, -REL_FUTURE, REL_PAST) + REL_FUTURE
    onehot = (r == idx).astype(F32)
    g = jnp.dot(tab_ref[...], onehot, precision=lax.Precision.HIGHEST,
                preferred_element_type=F32) * LOG2_E
    for j in range(A_PAIRS):
        halves = []
        for a in range(2):
            rows = jnp.broadcast_to(g[2 * j + a:2 * j + a + 1, :], (CHUNK, BIAS_W))
            halves.append(pltpu.roll(rows, 0, 1, stride=1, stride_axis=0))
        both = jnp.concatenate(halves, axis=0)
        col = lax.broadcasted_iota(jnp.int32, (2 * CHUNK, BAND2), 1)
        first = jnp.where(col < BAND, both, -jnp.inf)
        second = jnp.where(col >= CHUNK, pltpu.roll(both, CHUNK, 1), -jnp.inf)
        out_ref[j] = jnp.concatenate([first, second], axis=0)


def _attn_bias(rel_bias):
    return pl.pallas_call(
        _bias_kernel,
        out_shape=jax.ShapeDtypeStruct((A_PAIRS, 4 * CHUNK, BAND2), F32),
        compiler_params=pltpu.CompilerParams(vmem_limit_bytes=VMEM_LIMIT),
        name="attn_bias",
    )(rel_bias)


def _attn_pre_kernel(x_ref, g_ref, win_ref, gain_ref,
                     q_ref, k_ref, v_ref, sz_ref,
                     qk_scr):
    bm = x_ref.shape[0]
    xn = (_rms(x_ref[...]) * g_ref[...]).astype(BF16)
    qk_scr[...] = jnp.dot(xn, win_ref[:, 0:2 * A_INNER], preferred_element_type=F32)
    v_ref[...] = jnp.dot(xn, win_ref[:, 2 * A_INNER:3 * A_INNER],
                         preferred_element_type=F32).astype(BF16)
    z = jnp.dot(xn, win_ref[:, 3 * A_INNER:4 * A_INNER], preferred_element_type=F32)
    sz_ref[...] = (z * _sigmoid(z)).astype(BF16)

    first_head = lax.broadcasted_iota(jnp.int32, (bm, 2 * A_HD), 1) < A_HD
    for j in range(2 * A_PAIRS):
        lanes = slice(j * 2 * A_HD, (j + 1) * 2 * A_HD)
        xt = qk_scr[:, lanes]
        sq = xt * xt
        head_a = jnp.sum(jnp.where(first_head, sq, 0.0), axis=1, keepdims=True)
        head_b = jnp.sum(jnp.where(first_head, 0.0, sq), axis=1, keepdims=True)
        inv_a = lax.rsqrt(head_a * (1.0 / A_HD) + RMS_EPS)
        inv_b = lax.rsqrt(head_b * (1.0 / A_HD) + RMS_EPS)
        out = (xt * jnp.where(first_head, inv_a, inv_b) * gain_ref[:, lanes]).astype(BF16)
        if j < A_PAIRS:
            q_ref[:, lanes] = out
        else:
            k_ref[:, j * 2 * A_HD - A_INNER:(j + 1) * 2 * A_HD - A_INNER] = out


def _attn_pre(x, g, w_in, gain):
    B, T, D = x.shape
    bm = min(APRE_BM, T)
    nt = T // bm
    full = lambda *shape: pl.BlockSpec(shape, lambda b, t: (0,) * len(shape))
    tok = lambda width: pl.BlockSpec((None, bm, width), lambda b, t: (b, t, 0))
    o = jax.ShapeDtypeStruct((B, T, A_INNER), BF16)
    return pl.pallas_call(
        _attn_pre_kernel,
        grid=(B, nt),
        in_specs=[tok(D), full(1, D), full(D, 4 * A_INNER), full(1, 2 * A_INNER)],
        out_specs=(tok(A_INNER),) * 4,
        out_shape=(o, o, o, o),
        scratch_shapes=[pltpu.VMEM((bm, 2 * A_INNER), F32)],
        compiler_params=pltpu.CompilerParams(
            dimension_semantics=("parallel", "parallel"), vmem_limit_bytes=VMEM_LIMIT),
        name="attn_pre",
    )(x, g, w_in, gain)


def _attn_kernel(q_ref, kp_ref, kc_ref, vp_ref, vc_ref, sz_ref, x_ref, bias_ref, wout_ref,
                 out_ref,
                 o_s, s_even, s_odd, m_even, m_odd):
    i = pl.program_id(1)
    bm = q_ref.shape[0]
    group = 2 * CHUNK
    n_groups = bm // group
    first_head = lax.broadcasted_iota(jnp.int32, (CHUNK, 2 * A_HD), 1) < A_HD
    key_off = lax.broadcasted_iota(jnp.int32, (1, BAND2), 1)
    ones_tile = jnp.ones((bm, LANES), BF16)

    lanes = lambda j: slice(j * 2 * A_HD, (j + 1) * 2 * A_HD)
    rows = lambda g: slice(g * group, (g + 1) * group)
    old = lambda g: slice(g * group + bm - LEFT_CHUNKS * CHUNK, bm)
    new = lambda g: slice(0, (g + 1) * group)
    n_old = lambda g: LEFT_CHUNKS * CHUNK - g * group

    def scores(masked, g, j, s_ref, m_ref):
        q4 = q_ref[rows(g), lanes(j)].astype(F32)
        parts = []
        for c in range(2):
            qc = q4[c * CHUNK:(c + 1) * CHUNK, :]
            parts += [jnp.where(first_head, qc, 0.0), jnp.where(first_head, 0.0, qc)]
        wt = jnp.concatenate(parts, axis=0).astype(BF16)
        s = jnp.concatenate(
            [lax.dot_general(wt, kp_ref[old(g), lanes(j)], NT_DIMS, preferred_element_type=F32),
             lax.dot_general(wt, kc_ref[new(g), lanes(j)], NT_DIMS, preferred_element_type=F32)],
            axis=1) + bias_ref[j]
        if masked:
            valid = (g * group - LEFT_CHUNKS * CHUNK + key_off) >= 0
            s = jnp.where(valid, s, -jnp.inf)
        s_ref[j] = s
        m_ref[j] = jnp.broadcast_to(jnp.max(s, axis=1, keepdims=True), (2 * group, LANES))

    def finish(g, j, s_ref, m_ref):
        m = m_ref[j]
        e = jnp.exp2(s_ref[j] - jnp.concatenate([m] * (BAND2 // LANES), axis=1)).astype(BF16)
        v_old = jnp.concatenate([vp_ref[old(g), lanes(j)], ones_tile[old(g)]], axis=1)
        v_new = jnp.concatenate([vc_ref[new(g), lanes(j)], ones_tile[new(g)]], axis=1)
        r = (jnp.dot(e[:, 0:n_old(g)], v_old, preferred_element_type=F32)
             + jnp.dot(e[:, n_old(g):BAND2], v_new, preferred_element_type=F32))
        r = r[:, 0:2 * A_HD] * (1.0 / r[:, 2 * A_HD:4 * A_HD])
        o4 = jnp.concatenate(
            [jnp.where(first_head, r[2 * c * CHUNK:(2 * c + 1) * CHUNK, :],
                       r[(2 * c + 1) * CHUNK:(2 * c + 2) * CHUNK, :]) for c in range(2)], axis=0)
        o_s[rows(g), lanes(j)] = (o4 * sz_ref[rows(g), lanes(j)].astype(F32)).astype(BF16)

    even, odd = (s_even, m_even), (s_odd, m_odd)

    def run(masked):
        for j in range(A_PAIRS):
            scores(masked, 0, j, *even)
        for g in range(n_groups):
            cur, nxt = (even, odd) if g % 2 == 0 else (odd, even)
            for j in range(A_PAIRS):
                finish(g, j, *cur)
                if g + 1 < n_groups:
                    scores(masked, g + 1, j, *nxt)

    @pl.when(i == 0)
    def _():
        run(True)

    @pl.when(i > 0)
    def _():
        run(False)

    out_ref[...] = x_ref[...] + jnp.dot(o_s[...], wout_ref[...], preferred_element_type=F32)


def _attn(q, k, v, sz, x, bias, w_out):
    B, T, D = x.shape
    bm = ATT_BM
    nt = T // bm
    once = lambda *shape: pl.BlockSpec(shape, lambda b, t: (0,) * len(shape),
                                       pipeline_mode=pl.Buffered(1))
    cur = lambda width: pl.BlockSpec((None, bm, width), lambda b, t: (b, t, 0))
    prev = lambda width: pl.BlockSpec((None, bm, width), lambda b, t: (b, jnp.maximum(t - 1, 0), 0))
    return pl.pallas_call(
        _attn_kernel,
        grid=(B, nt),
        in_specs=[cur(A_INNER), prev(A_INNER), cur(A_INNER), prev(A_INNER), cur(A_INNER),
                  cur(A_INNER), cur(D), once(A_PAIRS, 4 * CHUNK, BAND2), once(A_INNER, D)],
        out_specs=cur(D),
        out_shape=jax.ShapeDtypeStruct((B, T, D), F32),
        scratch_shapes=[
            pltpu.VMEM((bm, A_INNER), BF16),
            pltpu.VMEM((A_PAIRS, 4 * CHUNK, BAND2), F32),
            pltpu.VMEM((A_PAIRS, 4 * CHUNK, BAND2), F32),
            pltpu.VMEM((A_PAIRS, 4 * CHUNK, LANES), F32),
            pltpu.VMEM((A_PAIRS, 4 * CHUNK, LANES), F32),
        ],
        compiler_params=pltpu.CompilerParams(
            dimension_semantics=("parallel", "parallel"), vmem_limit_bytes=VMEM_LIMIT),
        name="attn",
    )(q, k, k, v, v, sz, x, bias, w_out)


def kernel(x, norm_g, a_w_in, a_conv_w, a_conv_b, a_w_q, a_w_k, a_w_v, a_w_o, a_w_if, a_b_if,
           a_out_g, a_skip, a_w_out, b_w_in, b_q_g, b_k_g, b_rel_bias, b_w_out):
    assert x.shape[1] % ATT_BM == 0 and x.shape[2] == D_MODEL
    bf = lambda w: w.astype(BF16)
    pad_gate = lambda w: jnp.pad(w, ((0, 0), (0, GATE_PAD - 2 * M_HEADS)))

    x1 = _mlstm_layer(
        x, norm_g[0][None, :], bf(a_w_in[0]), a_conv_w[0], a_conv_b[0][None, :],
        bf(a_w_q[0]), bf(a_w_k[0]), bf(a_w_v[0]), bf(a_w_o[0]),
        bf(pad_gate(a_w_if[0])), pad_gate(a_b_if[0][None, :]), a_skip[0][None, :],
        a_out_g[0].reshape(1, M_INNER), bf(a_w_out[0]))

    bias = _attn_bias(b_rel_bias[0])
    gain = jnp.concatenate([jnp.tile(b_q_g[0], A_HEADS) * (A_HD ** -0.5 * LOG2_E),
                            jnp.tile(b_k_g[0], A_HEADS)])[None, :]
    q, k, v, sz = _attn_pre(x1, norm_g[1][None, :], bf(b_w_in[0]), gain)
    return _attn(q, k, v, sz, x1, bias, bf(b_w_out[0]))
```

```python
import functools

import jax
import jax.numpy as jnp
from jax import lax
from jax.experimental import pallas as pl
from jax.experimental.pallas import tpu as pltpu

F32 = jnp.float32
BF16 = jnp.bfloat16

RMS_EPS = 1e-6
CHUNK = 64
LOG2_E = 1.4426950408889634

D_MODEL = 1024
M_INNER = 2 * D_MODEL
M_HEADS = 4
M_DV = M_INNER // M_HEADS
M_DQK = M_DV // 2
CONV_K = 4

A_INNER = D_MODEL
A_HEADS = 16
A_HD = A_INNER // A_HEADS
A_PAIRS = A_HEADS // 2
LEFT_CHUNKS = 8
BAND = (LEFT_CHUNKS + 1) * CHUNK
REL_PAST = 256
REL_FUTURE = CHUNK - 1
REL_SIZE = REL_PAST + REL_FUTURE + 1

LANES = 128
SUBLANES = 8
GATE_PAD = LANES
VMEM_LIMIT = 56 * 1024 * 1024
FUSED_VMEM_LIMIT = 60 * 1024 * 1024
REC_L = 256
APRE_BM = 512
ATT_BM = LEFT_CHUNKS * CHUNK
BAND2 = BAND + CHUNK
BIAS_W = BAND2

NT_DIMS = (((1,), (1,)), ((), ()))
TN_DIMS = (((0,), (0,)), ((), ()))


def _sigmoid(x):
    return 0.5 * jnp.tanh(0.5 * x) + 0.5


def _silu(x):
    h = 0.5 * x
    return h + h * jnp.tanh(h)


def _log_sigmoid(x):
    return jnp.minimum(x, 0.0) - jnp.log1p(jnp.exp(-jnp.abs(x)))


def _rms(x, eps=RMS_EPS):
    return x * lax.rsqrt(jnp.mean(x * x, axis=-1, keepdims=True) + eps)


def _mlstm_layer_kernel(xp_ref, xc_ref, g_ref, win_ref, cw_ref, cb_ref, wq_ref, wk_ref, wv_ref, wo_ref,
                        wif_ref, bif_ref, skip_ref, og_ref, wout_ref,
                        out_ref,
                        ubuf, cb_scr, hq, hk, hv, ho, hcs, hsz, hgc, hgr,
                        ctx_ref, ctxb_ref, n_ref, m_ref, s_scr, qc_scr, qn_scr, rs_scr, p_scr, kw_scr,
                        pv_scr, cu_scr, y_scr,
                        *, blocks_per_row, n_blocks):
    step = pl.program_id(0)
    L = xp_ref.shape[0]
    wr = step % 2
    rd = 1 - wr
    prod_t = jnp.minimum(step, n_blocks - 1) % blocks_per_row
    cons_t = jnp.maximum(step - 1, 0) % blocks_per_row
    heads = range(M_HEADS)
    dv = lambda h: slice(h * M_DV, (h + 1) * M_DV)
    qk = lambda h: slice(h * M_DQK, (h + 1) * M_DQK)

    @pl.when(step == 0)
    def _():
        for ref in (hq, hk, hv, ho, hcs, hsz, hgc, hgr):
            ref[1] = jnp.zeros(ref.shape[1:], ref.dtype)

    @pl.when(prod_t == 0)
    def _():
        ubuf[0:SUBLANES, :] = jnp.zeros((SUBLANES, M_INNER), F32)

    @pl.when(cons_t == 0)
    def _():
        ctx_ref[...] = jnp.zeros(ctx_ref.shape, F32)
        ctxb_ref[...] = jnp.zeros(ctxb_ref.shape, BF16)
        n_ref[...] = jnp.zeros(n_ref.shape, F32)
        m_ref[...] = jnp.zeros(m_ref.shape, F32)

    for h in heads:
        qh = hq[rd, :, qk(h)]
        s_scr[h] = lax.dot_general(qh, hk[rd, :, qk(h)], NT_DIMS, preferred_element_type=F32)
        qc_scr[h] = jnp.dot(qh, ctxb_ref[h], preferred_element_type=F32)
        qn_scr[h] = jnp.broadcast_to(
            jnp.sum(qh.astype(F32) * n_ref[h], axis=1, keepdims=True), (L, LANES))

    xn = (_rms(xp_ref[...]) * g_ref[...]).astype(BF16)

    gc = hgc[rd]
    b = gc
    ar = pltpu.roll(gc, GATE_PAD - M_HEADS, 1)
    cm = pltpu.roll(gc, GATE_PAD - 2 * M_HEADS, 1)
    m_prev = m_ref[...]
    g_last = b[L - 1:L, :]
    a_c = g_last + ar
    m_new = jnp.maximum(g_last + m_prev, jnp.max(a_c, axis=0, keepdims=True))
    w_all = jnp.exp(a_c - m_new)
    decay = jnp.exp(g_last + m_prev - m_new)
    mx = jnp.maximum(m_prev, cm)
    inter_all = jnp.exp(m_prev - mx)
    enmt_all = jnp.exp(-(b + mx))
    m_ref[...] = m_new
    gr2 = hgr[rd] * LOG2_E
    mx2 = mx * LOG2_E
    tril = (lax.broadcasted_iota(jnp.int32, (L, L), 1)
            <= lax.broadcasted_iota(jnp.int32, (L, L), 0))

    for h in heads:
        d = jnp.where(tril, jnp.exp2(gr2[h:h + 1, :] - mx2[:, h:h + 1]), 0.0)
        p = s_scr[h] * d
        p_scr[h] = p.astype(BF16)
        rs_scr[h] = jnp.broadcast_to(jnp.sum(p, axis=1, keepdims=True), (L, LANES))
        kw = hk[rd, :, qk(h)].astype(F32) * w_all[:, h:h + 1]
        kw_scr[h] = kw.astype(BF16)
        n_ref[h] = decay[:, h:h + 1] * n_ref[h] + jnp.sum(kw, axis=0, keepdims=True)

    for h in heads:
        ubuf[SUBLANES:SUBLANES + L, dv(h)] = jnp.dot(xn, win_ref[:, dv(h)],
                                                     preferred_element_type=F32)

    def numerators_and_increments():
        for h in heads:
            vh = hv[rd, :, dv(h)]
            pv_scr[h] = jnp.dot(p_scr[h], vh, preferred_element_type=F32)
            cu_scr[h] = lax.dot_general(kw_scr[h], vh, TN_DIMS, preferred_element_type=F32)

    def conv_branch(h):
        u = ubuf[SUBLANES:SUBLANES + L, dv(h)]
        conv = cb_ref[:, dv(h)] + cw_ref[CONV_K - 1:CONV_K, dv(h)] * u
        for j in range(CONV_K - 1):
            shifted = ubuf[pl.ds(SUBLANES - (CONV_K - 1) + j, L), dv(h)]
            conv = conv + cw_ref[j:j + 1, dv(h)] * shifted
        c = _silu(conv)
        cb_scr[:, dv(h)] = c.astype(BF16)
        hcs[wr, :, dv(h)] = (skip_ref[:, dv(h)] * c).astype(BF16)

    def value_projections(h):
        ub16 = ubuf[SUBLANES:SUBLANES + L, dv(h)].astype(BF16)
        hv[wr, :, dv(h)] = jnp.dot(ub16, wv_ref[h], preferred_element_type=F32).astype(BF16)
        ho[wr, :, dv(h)] = _sigmoid(jnp.dot(ub16, wo_ref[h], preferred_element_type=F32)).astype(BF16)

    def key_projections(h):
        cb16 = cb_scr[:, dv(h)]
        hq[wr, :, qk(h)] = jnp.dot(cb16, wq_ref[h], preferred_element_type=F32).astype(BF16)
        hk[wr, :, qk(h)] = (jnp.dot(cb16, wk_ref[h], preferred_element_type=F32)
                            * (M_DQK ** -0.5)).astype(BF16)

    def gate_branch(h):
        z = jnp.dot(xn, win_ref[:, M_INNER + h * M_DV:M_INNER + (h + 1) * M_DV],
                    preferred_element_type=F32)
        hsz[wr, :, dv(h)] = _silu(z).astype(BF16)

    def normalise(h):
        inter = jnp.broadcast_to(inter_all[:, h:h + 1], (L, LANES))
        enmt = jnp.broadcast_to(enmt_all[:, h:h + 1], (L, LANES))
        num = pv_scr[h] + jnp.tile(inter, (1, M_DV // LANES)) * qc_scr[h]
        den = rs_scr[h] + inter * qn_scr[h]
        r = 1.0 / jnp.maximum(jnp.abs(den), enmt)
        ms = jnp.mean(num * num, axis=1, keepdims=True)
        scale = r * lax.rsqrt(r * r * ms + RMS_EPS)
        hn = (num * jnp.tile(scale, (1, M_DV // LANES)) * og_ref[:, dv(h)]).astype(BF16)
        y_scr[:, dv(h)] = (ho[rd, :, dv(h)] * hn + hcs[rd, :, dv(h)]) * hsz[rd, :, dv(h)]

    def project(h):
        part = jnp.dot(y_scr[:, dv(h)], wout_ref[dv(h), :], preferred_element_type=F32)
        if h == 0:
            out_ref[...] = xc_ref[...] + part
        else:
            out_ref[...] += part

    for h in heads:
        conv_branch(h)
    ubuf[0:SUBLANES, :] = ubuf[L:L + SUBLANES, :]
    numerators_and_increments()

    def gate_algebra():
        gates = bif_ref[...] + jnp.dot(cb_scr[...], wif_ref[...], preferred_element_type=F32)
        lf = _log_sigmoid(gates)
        lf_hi = lf.astype(BF16)
        rem = lf - lf_hi.astype(F32)
        lf_mid = rem.astype(BF16)
        lf_lo = (rem - lf_mid.astype(F32)).astype(BF16)
        trilb = tril.astype(BF16)
        cum = (jnp.dot(trilb, lf_hi, preferred_element_type=F32)
               + jnp.dot(trilb, lf_mid, preferred_element_type=F32)
               + jnp.dot(trilb, lf_lo, preferred_element_type=F32))
        nb = pltpu.roll(cum, GATE_PAD - M_HEADS, 1)
        nar = gates - nb
        row = lax.broadcasted_iota(jnp.int32, (L, GATE_PAD), 0)
        ncm = nar
        shift = 1
        while shift < L:
            ncm = jnp.maximum(ncm, jnp.where(row >= shift, pltpu.roll(ncm, shift, 0), -jnp.inf))
            shift *= 2
        lane = lax.broadcasted_iota(jnp.int32, (L, GATE_PAD), 1)
        hgc[wr] = jnp.where(lane < M_HEADS, nb,
                            jnp.where(lane < 2 * M_HEADS, pltpu.roll(nar, M_HEADS, 1),
                                      pltpu.roll(ncm, 2 * M_HEADS, 1)))
        hgr[wr] = nar.T[0:2 * M_HEADS, :]

    for h in heads:
        if h == M_HEADS // 2:
            gate_algebra()
        gate_branch(h)
        value_projections(h)
        key_projections(h)
        normalise(h)
        if h > 0:
            project(h - 1)
    project(M_HEADS - 1)

    for h in heads:
        ctx = decay[:, h:h + 1] * ctx_ref[h] + cu_scr[h]
        ctx_ref[h] = ctx
        ctxb_ref[h] = ctx.astype(BF16)


def _mlstm_layer(x, g, w_in, conv_w, conv_b, w_q, w_k, w_v, w_o, w_if, b_if, skip, out_g, w_out):
    B, T, D = x.shape
    L = REC_L
    nt = T // L
    n_blocks = B * nt
    once = lambda *shape: pl.BlockSpec(shape, lambda s: (0,) * len(shape), pipeline_mode=pl.Buffered(1))

    def prod_map(s):
        blk = jnp.minimum(s, n_blocks - 1)
        return (blk // nt, blk % nt, 0)

    def cons_map(s):
        blk = jnp.maximum(s - 1, 0)
        return (blk // nt, blk % nt, 0)

    hand = lambda width, dtype: pltpu.VMEM((2, L, width), dtype)
    return pl.pallas_call(
        functools.partial(_mlstm_layer_kernel, blocks_per_row=nt, n_blocks=n_blocks),
        grid=(n_blocks + 1,),
        in_specs=[
            pl.BlockSpec((None, L, D), prod_map), pl.BlockSpec((None, L, D), cons_map),
            once(1, D), once(D, 2 * M_INNER), once(CONV_K, M_INNER), once(1, M_INNER),
            once(M_HEADS, M_DV, M_DQK), once(M_HEADS, M_DV, M_DQK),
            once(M_HEADS, M_DV, M_DV), once(M_HEADS, M_DV, M_DV),
            once(M_INNER, GATE_PAD), once(1, GATE_PAD), once(1, M_INNER),
            once(1, M_INNER), once(M_INNER, D),
        ],
        out_specs=pl.BlockSpec((None, L, D), cons_map),
        out_shape=jax.ShapeDtypeStruct((B, T, D), F32),
        scratch_shapes=[
            pltpu.VMEM((SUBLANES + L, M_INNER), F32),
            pltpu.VMEM((L, M_INNER), BF16),
            hand(M_HEADS * M_DQK, BF16), hand(M_HEADS * M_DQK, BF16),
            hand(M_INNER, BF16), hand(M_INNER, BF16),
            hand(M_INNER, BF16), hand(M_INNER, BF16),
            hand(GATE_PAD, F32), pltpu.VMEM((2, 2 * M_HEADS, L), F32),
            pltpu.VMEM((M_HEADS, M_DQK, M_DV), F32),
            pltpu.VMEM((M_HEADS, M_DQK, M_DV), BF16),
            pltpu.VMEM((M_HEADS, 1, M_DQK), F32),
            pltpu.VMEM((1, LANES), F32),
            pltpu.VMEM((M_HEADS, L, L), F32),
            pltpu.VMEM((M_HEADS, L, M_DV), F32),
            pltpu.VMEM((M_HEADS, L, LANES), F32),
            pltpu.VMEM((M_HEADS, L, LANES), F32),
            pltpu.VMEM((M_HEADS, L, L), BF16),
            pltpu.VMEM((M_HEADS, L, M_DQK), BF16),
            pltpu.VMEM((M_HEADS, L, M_DV), F32),
            pltpu.VMEM((M_HEADS, M_DQK, M_DV), F32),
            pltpu.VMEM((L, M_INNER), BF16),
        ],
        compiler_params=pltpu.CompilerParams(
            dimension_semantics=("arbitrary",), vmem_limit_bytes=FUSED_VMEM_LIMIT),
        name="mlstm_layer",
    )(x, x, g, w_in, conv_w, conv_b, w_q, w_k, w_v, w_o, w_if, b_if, skip, out_g, w_out)


def _bias_kernel(tab_ref, out_ref):
    r = lax.broadcasted_iota(jnp.int32, (REL_SIZE, BIAS_W), 0)
    e = lax.broadcasted_iota(jnp.int32, (REL_SIZE, BIAS_W), 1)
    e = jnp.where(e >= BAND, e - BIAS_W, e)
    idx = jnp.clip(LEFT_CHUNKS * CHUNK - e, -REL_FUTURE, REL_PAST) + REL_FUTURE
    onehot = (r == idx).astype(F32)
    g = jnp.dot(tab_ref[...], onehot, precision=lax.Precision.HIGHEST,
                preferred_element_type=F32) * LOG2_E
    for j in range(A_PAIRS):
        halves = []
        for a in range(2):
            rows = jnp.broadcast_to(g[2 * j + a:2 * j + a + 1, :], (CHUNK, BIAS_W))
            halves.append(pltpu.roll(rows, 0, 1, stride=1, stride_axis=0))
        both = jnp.concatenate(halves, axis=0)
        col = lax.broadcasted_iota(jnp.int32, (2 * CHUNK, BAND2), 1)
        first = jnp.where(col < BAND, both, -jnp.inf)
        second = jnp.where(col >= CHUNK, pltpu.roll(both, CHUNK, 1), -jnp.inf)
        out_ref[j] = jnp.concatenate([first, second], axis=0)


def _attn_bias(rel_bias):
    return pl.pallas_call(
        _bias_kernel,
        out_shape=jax.ShapeDtypeStruct((A_PAIRS, 4 * CHUNK, BAND2), F32),
        compiler_params=pltpu.CompilerParams(vmem_limit_bytes=VMEM_LIMIT),
        name="attn_bias",
    )(rel_bias)


def _attn_pre_kernel(x_ref, g_ref, win_ref, gain_ref,
                     q_ref, k_ref, v_ref, sz_ref,
                     qk_scr):
    bm = x_ref.shape[0]
    xn = (_rms(x_ref[...]) * g_ref[...]).astype(BF16)
    qk_scr[...] = jnp.dot(xn, win_ref[:, 0:2 * A_INNER], preferred_element_type=F32)
    v_ref[...] = jnp.dot(xn, win_ref[:, 2 * A_INNER:3 * A_INNER],
                         preferred_element_type=F32).astype(BF16)
    z = jnp.dot(xn, win_ref[:, 3 * A_INNER:4 * A_INNER], preferred_element_type=F32)
    sz_ref[...] = _silu(z).astype(BF16)

    first_head = lax.broadcasted_iota(jnp.int32, (bm, 2 * A_HD), 1) < A_HD
    for j in range(2 * A_PAIRS):
        lanes = slice(j * 2 * A_HD, (j + 1) * 2 * A_HD)
        xt = qk_scr[:, lanes]
        sq = xt * xt
        head_a = jnp.sum(jnp.where(first_head, sq, 0.0), axis=1, keepdims=True)
        head_b = jnp.sum(jnp.where(first_head, 0.0, sq), axis=1, keepdims=True)
        inv_a = lax.rsqrt(head_a * (1.0 / A_HD) + RMS_EPS)
        inv_b = lax.rsqrt(head_b * (1.0 / A_HD) + RMS_EPS)
        out = (xt * jnp.where(first_head, inv_a, inv_b) * gain_ref[:, lanes]).astype(BF16)
        if j < A_PAIRS:
            q_ref[:, lanes] = out
        else:
            k_ref[:, j * 2 * A_HD - A_INNER:(j + 1) * 2 * A_HD - A_INNER] = out


def _attn_pre(x, g, w_in, gain):
    B, T, D = x.shape
    bm = min(APRE_BM, T)
    nt = T // bm
    full = lambda *shape: pl.BlockSpec(shape, lambda b, t: (0,) * len(shape))
    tok = lambda width: pl.BlockSpec((None, bm, width), lambda b, t: (b, t, 0))
    o = jax.ShapeDtypeStruct((B, T, A_INNER), BF16)
    return pl.pallas_call(
        _attn_pre_kernel,
        grid=(B, nt),
        in_specs=[tok(D), full(1, D), full(D, 4 * A_INNER), full(1, 2 * A_INNER)],
        out_specs=(tok(A_INNER),) * 4,
        out_shape=(o, o, o, o),
        scratch_shapes=[pltpu.VMEM((bm, 2 * A_INNER), F32)],
        compiler_params=pltpu.CompilerParams(
            dimension_semantics=("parallel", "parallel"), vmem_limit_bytes=VMEM_LIMIT),
        name="attn_pre",
    )(x, g, w_in, gain)


def _attn_kernel(q_ref, kp_ref, kc_ref, vp_ref, vc_ref, sz_ref, x_ref, bias_ref, wout_ref,
                 out_ref,
                 o_s, s_even, s_odd, m_even, m_odd):
    i = pl.program_id(1)
    bm = q_ref.shape[0]
    group = 2 * CHUNK
    n_groups = bm // group
    first_head = lax.broadcasted_iota(jnp.int32, (CHUNK, 2 * A_HD), 1) < A_HD
    key_off = lax.broadcasted_iota(jnp.int32, (1, BAND2), 1)
    ones_tile = jnp.ones((bm, LANES), BF16)

    lanes = lambda j: slice(j * 2 * A_HD, (j + 1) * 2 * A_HD)
    rows = lambda g: slice(g * group, (g + 1) * group)
    old = lambda g: slice(g * group + bm - LEFT_CHUNKS * CHUNK, bm)
    new = lambda g: slice(0, (g + 1) * group)
    n_old = lambda g: LEFT_CHUNKS * CHUNK - g * group

    def scores(masked, g, j, s_ref, m_ref):
        q4 = q_ref[rows(g), lanes(j)].astype(F32)
        parts = []
        for c in range(2):
            qc = q4[c * CHUNK:(c + 1) * CHUNK, :]
            parts += [jnp.where(first_head, qc, 0.0), jnp.where(first_head, 0.0, qc)]
        wt = jnp.concatenate(parts, axis=0).astype(BF16)
        s = jnp.concatenate(
            [lax.dot_general(wt, kp_ref[old(g), lanes(j)], NT_DIMS, preferred_element_type=F32),
             lax.dot_general(wt, kc_ref[new(g), lanes(j)], NT_DIMS, preferred_element_type=F32)],
            axis=1) + bias_ref[j]
        if masked:
            valid = (g * group - LEFT_CHUNKS * CHUNK + key_off) >= 0
            s = jnp.where(valid, s, -jnp.inf)
        s_ref[j] = s
        m_ref[j] = jnp.broadcast_to(jnp.max(s, axis=1, keepdims=True), (2 * group, LANES))

    def finish(g, j, s_ref, m_ref):
        m = m_ref[j]
        e = jnp.exp2(s_ref[j] - jnp.concatenate([m] * (BAND2 // LANES), axis=1)).astype(BF16)
        v_old = jnp.concatenate([vp_ref[old(g), lanes(j)], ones_tile[old(g)]], axis=1)
        v_new = jnp.concatenate([vc_ref[new(g), lanes(j)], ones_tile[new(g)]], axis=1)
        r = (jnp.dot(e[:, 0:n_old(g)], v_old, preferred_element_type=F32)
             + jnp.dot(e[:, n_old(g):BAND2], v_new, preferred_element_type=F32))
        r = r[:, 0:2 * A_HD] * (1.0 / r[:, 2 * A_HD:4 * A_HD])
        o4 = jnp.concatenate(
            [jnp.where(first_head, r[2 * c * CHUNK:(2 * c + 1) * CHUNK, :],
                       r[(2 * c + 1) * CHUNK:(2 * c + 2) * CHUNK, :]) for c in range(2)], axis=0)
        o_s[rows(g), lanes(j)] = (o4 * sz_ref[rows(g), lanes(j)].astype(F32)).astype(BF16)

    even, odd = (s_even, m_even), (s_odd, m_odd)

    def run(masked):
        for j in range(A_PAIRS):
            scores(masked, 0, j, *even)
        for g in range(n_groups):
            cur, nxt = (even, odd) if g % 2 == 0 else (odd, even)
            for j in range(A_PAIRS):
                finish(g, j, *cur)
                if g + 1 < n_groups:
                    scores(masked, g + 1, j, *nxt)

    @pl.when(i == 0)
    def _():
        run(True)

    @pl.when(i > 0)
    def _():
        run(False)

    out_ref[...] = x_ref[...] + jnp.dot(o_s[...], wout_ref[...], preferred_element_type=F32)


def _attn(q, k, v, sz, x, bias, w_out):
    B, T, D = x.shape
    bm = ATT_BM
    nt = T // bm
    once = lambda *shape: pl.BlockSpec(shape, lambda b, t: (0,) * len(shape),
                                       pipeline_mode=pl.Buffered(1))
    cur = lambda width: pl.BlockSpec((None, bm, width), lambda b, t: (b, t, 0))
    prev = lambda width: pl.BlockSpec((None, bm, width), lambda b, t: (b, jnp.maximum(t - 1, 0), 0))
    return pl.pallas_call(
        _attn_kernel,
        grid=(B, nt),
        in_specs=[cur(A_INNER), prev(A_INNER), cur(A_INNER), prev(A_INNER), cur(A_INNER),
                  cur(A_INNER), cur(D), once(A_PAIRS, 4 * CHUNK, BAND2), once(A_INNER, D)],
        out_specs=cur(D),
        out_shape=jax.ShapeDtypeStruct((B, T, D), F32),
        scratch_shapes=[
            pltpu.VMEM((bm, A_INNER), BF16),
            pltpu.VMEM((A_PAIRS, 4 * CHUNK, BAND2), F32),
            pltpu.VMEM((A_PAIRS, 4 * CHUNK, BAND2), F32),
            pltpu.VMEM((A_PAIRS, 4 * CHUNK, LANES), F32),
            pltpu.VMEM((A_PAIRS, 4 * CHUNK, LANES), F32),
        ],
        compiler_params=pltpu.CompilerParams(
            dimension_semantics=("parallel", "parallel"), vmem_limit_bytes=VMEM_LIMIT),
        name="attn",
    )(q, k, k, v, v, sz, x, bias, w_out)


def kernel(x, norm_g, a_w_in, a_conv_w, a_conv_b, a_w_q, a_w_k, a_w_v, a_w_o, a_w_if, a_b_if,
           a_out_g, a_skip, a_w_out, b_w_in, b_q_g, b_k_g, b_rel_bias, b_w_out):
    assert x.shape[1] % ATT_BM == 0 and x.shape[2] == D_MODEL
    bf = lambda w: w.astype(BF16)
    pad_gate = lambda w: jnp.pad(w, ((0, 0), (0, GATE_PAD - 2 * M_HEADS)))

    x1 = _mlstm_layer(
        x, norm_g[0][None, :], bf(a_w_in[0]), a_conv_w[0], a_conv_b[0][None, :],
        bf(a_w_q[0]), bf(a_w_k[0]), bf(a_w_v[0]), bf(a_w_o[0]),
        bf(pad_gate(a_w_if[0])), pad_gate(a_b_if[0][None, :]), a_skip[0][None, :],
        a_out_g[0].reshape(1, M_INNER), bf(a_w_out[0]))

    bias = _attn_bias(b_rel_bias[0])
    gain = jnp.concatenate([jnp.tile(b_q_g[0], A_HEADS) * (A_HD ** -0.5 * LOG2_E),
                            jnp.tile(b_k_g[0], A_HEADS)])[None, :]
    q, k, v, sz = _attn_pre(x1, norm_g[1][None, :], bf(b_w_in[0]), gain)
    return _attn(q, k, v, sz, x1, bias, bf(b_w_out[0]))
```

```python
import functools

import jax
import jax.numpy as jnp
from jax import lax
from jax.experimental import pallas as pl
from jax.experimental.pallas import tpu as pltpu

F32 = jnp.float32
BF16 = jnp.bfloat16

RMS_EPS = 1e-6
CHUNK = 64
LOG2_E = 1.4426950408889634

D_MODEL = 1024
M_INNER = 2 * D_MODEL
M_HEADS = 4
M_DV = M_INNER // M_HEADS
M_DQK = M_DV // 2
CONV_K = 4

A_INNER = D_MODEL
A_HEADS = 16
A_HD = A_INNER // A_HEADS
A_PAIRS = A_HEADS // 2
LEFT_CHUNKS = 8
BAND = (LEFT_CHUNKS + 1) * CHUNK
REL_PAST = 256
REL_FUTURE = CHUNK - 1
REL_SIZE = REL_PAST + REL_FUTURE + 1

LANES = 128
SUBLANES = 8
GATE_PAD = LANES
VMEM_LIMIT = 56 * 1024 * 1024
FUSED_VMEM_LIMIT = 60 * 1024 * 1024
REC_L = 256
APRE_BM = 1024
ATT_BM = LEFT_CHUNKS * CHUNK
BAND2 = BAND + CHUNK
BIAS_W = BAND2

NT_DIMS = (((1,), (1,)), ((), ()))
TN_DIMS = (((0,), (0,)), ((), ()))


def _sigmoid(x):
    return 0.5 * jnp.tanh(0.5 * x) + 0.5


def _silu(x):
    h = 0.5 * x
    return h + h * jnp.tanh(h)


def _log_sigmoid(x):
    return jnp.minimum(x, 0.0) - jnp.log1p(jnp.exp(-jnp.abs(x)))


def _rms(x, eps=RMS_EPS):
    return x * lax.rsqrt(jnp.mean(x * x, axis=-1, keepdims=True) + eps)


def _mlstm_layer_kernel(xp_ref, xc_ref, g_ref, win_ref, cw_ref, cb_ref, wq_ref, wk_ref, wv_ref, wo_ref,
                        wif_ref, bif_ref, skip_ref, og_ref, wout_ref,
                        out_ref,
                        ubuf, cb_scr, hq, hk, hv, ho, hcs, hsz, hgc, hgr,
                        ctx_ref, ctxb_ref, n_ref, m_ref, s_scr, qc_scr, qn_scr, rs_scr, p_scr, kw_scr,
                        pv_scr, cu_scr, y_scr,
                        *, blocks_per_row, n_blocks):
    step = pl.program_id(0)
    L = xp_ref.shape[0]
    wr = step % 2
    rd = 1 - wr
    prod_t = jnp.minimum(step, n_blocks - 1) % blocks_per_row
    cons_t = jnp.maximum(step - 1, 0) % blocks_per_row
    heads = range(M_HEADS)
    dv = lambda h: slice(h * M_DV, (h + 1) * M_DV)
    qk = lambda h: slice(h * M_DQK, (h + 1) * M_DQK)

    @pl.when(step == 0)
    def _():
        for ref in (hq, hk, hv, ho, hcs, hsz, hgc, hgr):
            ref[1] = jnp.zeros(ref.shape[1:], ref.dtype)

    @pl.when(prod_t == 0)
    def _():
        ubuf[0:SUBLANES, :] = jnp.zeros((SUBLANES, M_INNER), F32)

    @pl.when(cons_t == 0)
    def _():
        ctx_ref[...] = jnp.zeros(ctx_ref.shape, F32)
        ctxb_ref[...] = jnp.zeros(ctxb_ref.shape, BF16)
        n_ref[...] = jnp.zeros(n_ref.shape, F32)
        m_ref[...] = jnp.zeros(m_ref.shape, F32)

    for h in heads:
        qh = hq[rd, :, qk(h)]
        s_scr[h] = lax.dot_general(qh, hk[rd, :, qk(h)], NT_DIMS, preferred_element_type=F32)
        qc_scr[h] = jnp.dot(qh, ctxb_ref[h], preferred_element_type=F32)
        qn_scr[h] = jnp.broadcast_to(
            jnp.sum(qh.astype(F32) * n_ref[h], axis=1, keepdims=True), (L, LANES))

    xn = (_rms(xp_ref[...]) * g_ref[...]).astype(BF16)

    gc = hgc[rd]
    b = gc
    ar = pltpu.roll(gc, GATE_PAD - M_HEADS, 1)
    cm = pltpu.roll(gc, GATE_PAD - 2 * M_HEADS, 1)
    m_prev = m_ref[...]
    g_last = b[L - 1:L, :]
    a_c = g_last + ar
    m_new = jnp.maximum(g_last + m_prev, jnp.max(a_c, axis=0, keepdims=True))
    w_all = jnp.exp(a_c - m_new)
    decay = jnp.exp(g_last + m_prev - m_new)
    mx = jnp.maximum(m_prev, cm)
    inter_all = jnp.exp(m_prev - mx)
    enmt_all = jnp.exp(-(b + mx))
    m_ref[...] = m_new
    gr2 = hgr[rd] * LOG2_E
    mx2 = mx * LOG2_E
    tril = (lax.broadcasted_iota(jnp.int32, (L, L), 1)
            <= lax.broadcasted_iota(jnp.int32, (L, L), 0))

    for h in heads:
        d = jnp.where(tril, jnp.exp2(gr2[h:h + 1, :] - mx2[:, h:h + 1]), 0.0)
        p = s_scr[h] * d
        p_scr[h] = p.astype(BF16)
        rs_scr[h] = jnp.broadcast_to(jnp.sum(p, axis=1, keepdims=True), (L, LANES))
        kw = hk[rd, :, qk(h)].astype(F32) * w_all[:, h:h + 1]
        kw_scr[h] = kw.astype(BF16)
        n_ref[h] = decay[:, h:h + 1] * n_ref[h] + jnp.sum(kw, axis=0, keepdims=True)

    for h in heads:
        ubuf[SUBLANES:SUBLANES + L, dv(h)] = jnp.dot(xn, win_ref[:, dv(h)],
                                                     preferred_element_type=F32)

    def numerators_and_increments():
        for h in heads:
            vh = hv[rd, :, dv(h)]
            pv_scr[h] = jnp.dot(p_scr[h], vh, preferred_element_type=F32)
            cu_scr[h] = lax.dot_general(kw_scr[h], vh, TN_DIMS, preferred_element_type=F32)

    def conv_branch(h):
        u = ubuf[SUBLANES:SUBLANES + L, dv(h)]
        conv = cb_ref[:, dv(h)] + cw_ref[CONV_K - 1:CONV_K, dv(h)] * u
        for j in range(CONV_K - 1):
            shifted = ubuf[pl.ds(SUBLANES - (CONV_K - 1) + j, L), dv(h)]
            conv = conv + cw_ref[j:j + 1, dv(h)] * shifted
        c = _silu(conv)
        cb_scr[:, dv(h)] = c.astype(BF16)
        hcs[wr, :, dv(h)] = (skip_ref[:, dv(h)] * c).astype(BF16)

    def value_projections(h):
        ub16 = ubuf[SUBLANES:SUBLANES + L, dv(h)].astype(BF16)
        hv[wr, :, dv(h)] = jnp.dot(ub16, wv_ref[h], preferred_element_type=F32).astype(BF16)
        ho[wr, :, dv(h)] = _sigmoid(jnp.dot(ub16, wo_ref[h], preferred_element_type=F32)).astype(BF16)

    def key_projections(h):
        cb16 = cb_scr[:, dv(h)]
        hq[wr, :, qk(h)] = jnp.dot(cb16, wq_ref[h], preferred_element_type=F32).astype(BF16)
        hk[wr, :, qk(h)] = (jnp.dot(cb16, wk_ref[h], preferred_element_type=F32)
                            * (M_DQK ** -0.5)).astype(BF16)

    def gate_branch(h):
        z = jnp.dot(xn, win_ref[:, M_INNER + h * M_DV:M_INNER + (h + 1) * M_DV],
                    preferred_element_type=F32)
        hsz[wr, :, dv(h)] = _silu(z).astype(BF16)

    def normalise(h):
        inter = jnp.broadcast_to(inter_all[:, h:h + 1], (L, LANES))
        enmt = jnp.broadcast_to(enmt_all[:, h:h + 1], (L, LANES))
        num = pv_scr[h] + jnp.tile(inter, (1, M_DV // LANES)) * qc_scr[h]
        den = rs_scr[h] + inter * qn_scr[h]
        r = 1.0 / jnp.maximum(jnp.abs(den), enmt)
        ms = jnp.mean(num * num, axis=1, keepdims=True)
        scale = r * lax.rsqrt(r * r * ms + RMS_EPS)
        hn = (num * jnp.tile(scale, (1, M_DV // LANES)) * og_ref[:, dv(h)]).astype(BF16)
        y_scr[:, dv(h)] = (ho[rd, :, dv(h)] * hn + hcs[rd, :, dv(h)]) * hsz[rd, :, dv(h)]

    def project(h):
        part = jnp.dot(y_scr[:, dv(h)], wout_ref[dv(h), :], preferred_element_type=F32)
        if h == 0:
            out_ref[...] = xc_ref[...] + part
        else:
            out_ref[...] += part

    for h in heads:
        conv_branch(h)
    ubuf[0:SUBLANES, :] = ubuf[L:L + SUBLANES, :]
    numerators_and_increments()

    def gate_algebra():
        gates = bif_ref[...] + jnp.dot(cb_scr[...], wif_ref[...], preferred_element_type=F32)
        lf = _log_sigmoid(gates)
        lf_hi = lf.astype(BF16)
        rem = lf - lf_hi.astype(F32)
        lf_mid = rem.astype(BF16)
        lf_lo = (rem - lf_mid.astype(F32)).astype(BF16)
        trilb = tril.astype(BF16)
        cum = (jnp.dot(trilb, lf_hi, preferred_element_type=F32)
               + jnp.dot(trilb, lf_mid, preferred_element_type=F32)
               + jnp.dot(trilb, lf_lo, preferred_element_type=F32))
        nb = pltpu.roll(cum, GATE_PAD - M_HEADS, 1)
        nar = gates - nb
        row = lax.broadcasted_iota(jnp.int32, (L, GATE_PAD), 0)
        ncm = nar
        shift = 1
        while shift < L:
            ncm = jnp.maximum(ncm, jnp.where(row >= shift, pltpu.roll(ncm, shift, 0), -jnp.inf))
            shift *= 2
        lane = lax.broadcasted_iota(jnp.int32, (L, GATE_PAD), 1)
        hgc[wr] = jnp.where(lane < M_HEADS, nb,
                            jnp.where(lane < 2 * M_HEADS, pltpu.roll(nar, M_HEADS, 1),
                                      pltpu.roll(ncm, 2 * M_HEADS, 1)))
        hgr[wr] = nar.T[0:2 * M_HEADS, :]

    for h in heads:
        if h == M_HEADS // 2:
            gate_algebra()
        gate_branch(h)
        value_projections(h)
        key_projections(h)
        normalise(h)
        if h > 0:
            project(h - 1)
    project(M_HEADS - 1)

    for h in heads:
        ctx = decay[:, h:h + 1] * ctx_ref[h] + cu_scr[h]
        ctx_ref[h] = ctx
        ctxb_ref[h] = ctx.astype(BF16)


def _mlstm_layer(x, g, w_in, conv_w, conv_b, w_q, w_k, w_v, w_o, w_if, b_if, skip, out_g, w_out):
    B, T, D = x.shape
    L = REC_L
    nt = T // L
    n_blocks = B * nt
    once = lambda *shape: pl.BlockSpec(shape, lambda s: (0,) * len(shape), pipeline_mode=pl.Buffered(1))

    def prod_map(s):
        blk = jnp.minimum(s, n_blocks - 1)
        return (blk // nt, blk % nt, 0)

    def cons_map(s):
        blk = jnp.maximum(s - 1, 0)
        return (blk // nt, blk % nt, 0)

    hand = lambda width, dtype: pltpu.VMEM((2, L, width), dtype)
    return pl.pallas_call(
        functools.partial(_mlstm_layer_kernel, blocks_per_row=nt, n_blocks=n_blocks),
        grid=(n_blocks + 1,),
        in_specs=[
            pl.BlockSpec((None, L, D), prod_map), pl.BlockSpec((None, L, D), cons_map),
            once(1, D), once(D, 2 * M_INNER), once(CONV_K, M_INNER), once(1, M_INNER),
            once(M_HEADS, M_DV, M_DQK), once(M_HEADS, M_DV, M_DQK),
            once(M_HEADS, M_DV, M_DV), once(M_HEADS, M_DV, M_DV),
            once(M_INNER, GATE_PAD), once(1, GATE_PAD), once(1, M_INNER),
            once(1, M_INNER), once(M_INNER, D),
        ],
        out_specs=pl.BlockSpec((None, L, D), cons_map),
        out_shape=jax.ShapeDtypeStruct((B, T, D), F32),
        scratch_shapes=[
            pltpu.VMEM((SUBLANES + L, M_INNER), F32),
            pltpu.VMEM((L, M_INNER), BF16),
            hand(M_HEADS * M_DQK, BF16), hand(M_HEADS * M_DQK, BF16),
            hand(M_INNER, BF16), hand(M_INNER, BF16),
            hand(M_INNER, BF16), hand(M_INNER, BF16),
            hand(GATE_PAD, F32), pltpu.VMEM((2, 2 * M_HEADS, L), F32),
            pltpu.VMEM((M_HEADS, M_DQK, M_DV), F32),
            pltpu.VMEM((M_HEADS, M_DQK, M_DV), BF16),
            pltpu.VMEM((M_HEADS, 1, M_DQK), F32),
            pltpu.VMEM((1, LANES), F32),
            pltpu.VMEM((M_HEADS, L, L), F32),
            pltpu.VMEM((M_HEADS, L, M_DV), F32),
            pltpu.VMEM((M_HEADS, L, LANES), F32),
            pltpu.VMEM((M_HEADS, L, LANES), F32),
            pltpu.VMEM((M_HEADS, L, L), BF16),
            pltpu.VMEM((M_HEADS, L, M_DQK), BF16),
            pltpu.VMEM((M_HEADS, L, M_DV), F32),
            pltpu.VMEM((M_HEADS, M_DQK, M_DV), F32),
            pltpu.VMEM((L, M_INNER), BF16),
        ],
        compiler_params=pltpu.CompilerParams(
            dimension_semantics=("arbitrary",), vmem_limit_bytes=FUSED_VMEM_LIMIT),
        name="mlstm_layer",
    )(x, x, g, w_in, conv_w, conv_b, w_q, w_k, w_v, w_o, w_if, b_if, skip, out_g, w_out)


def _bias_kernel(tab_ref, out_ref):
    r = lax.broadcasted_iota(jnp.int32, (REL_SIZE, BIAS_W), 0)
    e = lax.broadcasted_iota(jnp.int32, (REL_SIZE, BIAS_W), 1)
    e = jnp.where(e >= BAND, e - BIAS_W, e)
    idx = jnp.clip(LEFT_CHUNKS * CHUNK - e, -REL_FUTURE, REL_PAST) + REL_FUTURE
    onehot = (r == idx).astype(F32)
    g = jnp.dot(tab_ref[...], onehot, precision=lax.Precision.HIGHEST,
                preferred_element_type=F32) * LOG2_E
    for j in range(A_PAIRS):
        halves = []
        for a in range(2):
            rows = jnp.broadcast_to(g[2 * j + a:2 * j + a + 1, :], (CHUNK, BIAS_W))
            halves.append(pltpu.roll(rows, 0, 1, stride=1, stride_axis=0))
        both = jnp.concatenate(halves, axis=0)
        col = lax.broadcasted_iota(jnp.int32, (2 * CHUNK, BAND2), 1)
        first = jnp.where(col < BAND, both, -jnp.inf)
        second = jnp.where(col >= CHUNK, pltpu.roll(both, CHUNK, 1), -jnp.inf)
        out_ref[j] = jnp.concatenate([first, second], axis=0)


def _attn_bias(rel_bias):
    return pl.pallas_call(
        _bias_kernel,
        out_shape=jax.ShapeDtypeStruct((A_PAIRS, 4 * CHUNK, BAND2), F32),
        compiler_params=pltpu.CompilerParams(vmem_limit_bytes=VMEM_LIMIT),
        name="attn_bias",
    )(rel_bias)


def _attn_pre_kernel(x_ref, g_ref, win_ref, gain_ref,
                     q_ref, k_ref, v_ref, sz_ref,
                     qk_scr):
    bm = x_ref.shape[0]
    xn = (_rms(x_ref[...]) * g_ref[...]).astype(BF16)
    qk_scr[...] = jnp.dot(xn, win_ref[:, 0:2 * A_INNER], preferred_element_type=F32)
    v_ref[...] = jnp.dot(xn, win_ref[:, 2 * A_INNER:3 * A_INNER],
                         preferred_element_type=F32).astype(BF16)
    z = jnp.dot(xn, win_ref[:, 3 * A_INNER:4 * A_INNER], preferred_element_type=F32)
    sz_ref[...] = _silu(z).astype(BF16)

    first_head = lax.broadcasted_iota(jnp.int32, (bm, 2 * A_HD), 1) < A_HD
    for j in range(2 * A_PAIRS):
        lanes = slice(j * 2 * A_HD, (j + 1) * 2 * A_HD)
        xt = qk_scr[:, lanes]
        sq = xt * xt
        head_a = jnp.sum(jnp.where(first_head, sq, 0.0), axis=1, keepdims=True)
        head_b = jnp.sum(jnp.where(first_head, 0.0, sq), axis=1, keepdims=True)
        inv_a = lax.rsqrt(head_a * (1.0 / A_HD) + RMS_EPS)
        inv_b = lax.rsqrt(head_b * (1.0 / A_HD) + RMS_EPS)
        out = (xt * jnp.where(first_head, inv_a, inv_b) * gain_ref[:, lanes]).astype(BF16)
        if j < A_PAIRS:
            q_ref[:, lanes] = out
        else:
            k_ref[:, j * 2 * A_HD - A_INNER:(j + 1) * 2 * A_HD - A_INNER] = out


def _attn_pre(x, g, w_in, gain):
    B, T, D = x.shape
    bm = min(APRE_BM, T)
    nt = T // bm
    full = lambda *shape: pl.BlockSpec(shape, lambda b, t: (0,) * len(shape))
    tok = lambda width: pl.BlockSpec((None, bm, width), lambda b, t: (b, t, 0))
    o = jax.ShapeDtypeStruct((B, T, A_INNER), BF16)
    return pl.pallas_call(
        _attn_pre_kernel,
        grid=(B, nt),
        in_specs=[tok(D), full(1, D), full(D, 4 * A_INNER), full(1, 2 * A_INNER)],
        out_specs=(tok(A_INNER),) * 4,
        out_shape=(o, o, o, o),
        scratch_shapes=[pltpu.VMEM((bm, 2 * A_INNER), F32)],
        compiler_params=pltpu.CompilerParams(
            dimension_semantics=("parallel", "parallel"), vmem_limit_bytes=VMEM_LIMIT),
        name="attn_pre",
    )(x, g, w_in, gain)


def _attn_kernel(q_ref, kp_ref, kc_ref, vp_ref, vc_ref, sz_ref, x_ref, bias_ref, wout_ref,
                 out_ref,
                 o_s, s_even, s_odd, m_even, m_odd):
    i = pl.program_id(1)
    bm = q_ref.shape[0]
    group = 2 * CHUNK
    n_groups = bm // group
    first_head = lax.broadcasted_iota(jnp.int32, (CHUNK, 2 * A_HD), 1) < A_HD
    key_off = lax.broadcasted_iota(jnp.int32, (1, BAND2), 1)
    ones_tile = jnp.ones((bm, LANES), BF16)

    lanes = lambda j: slice(j * 2 * A_HD, (j + 1) * 2 * A_HD)
    rows = lambda g: slice(g * group, (g + 1) * group)
    old = lambda g: slice(g * group + bm - LEFT_CHUNKS * CHUNK, bm)
    new = lambda g: slice(0, (g + 1) * group)
    n_old = lambda g: LEFT_CHUNKS * CHUNK - g * group

    def scores(masked, g, j, s_ref, m_ref):
        q4 = q_ref[rows(g), lanes(j)].astype(F32)
        parts = []
        for c in range(2):
            qc = q4[c * CHUNK:(c + 1) * CHUNK, :]
            parts += [jnp.where(first_head, qc, 0.0), jnp.where(first_head, 0.0, qc)]
        wt = jnp.concatenate(parts, axis=0).astype(BF16)
        s = jnp.concatenate(
            [lax.dot_general(wt, kp_ref[old(g), lanes(j)], NT_DIMS, preferred_element_type=F32),
             lax.dot_general(wt, kc_ref[new(g), lanes(j)], NT_DIMS, preferred_element_type=F32)],
            axis=1) + bias_ref[j]
        if masked:
            valid = (g * group - LEFT_CHUNKS * CHUNK + key_off) >= 0
            s = jnp.where(valid, s, -jnp.inf)
        s_ref[j] = s
        m_ref[j] = jnp.broadcast_to(jnp.max(s, axis=1, keepdims=True), (2 * group, LANES))

    def finish(g, j, s_ref, m_ref):
        m = m_ref[j]
        e = jnp.exp2(s_ref[j] - jnp.concatenate([m] * (BAND2 // LANES), axis=1)).astype(BF16)
        v_old = jnp.concatenate([vp_ref[old(g), lanes(j)], ones_tile[old(g)]], axis=1)
        v_new = jnp.concatenate([vc_ref[new(g), lanes(j)], ones_tile[new(g)]], axis=1)
        r = (jnp.dot(e[:, 0:n_old(g)], v_old, preferred_element_type=F32)
             + jnp.dot(e[:, n_old(g):BAND2], v_new, preferred_element_type=F32))
        r = r[:, 0:2 * A_HD] * (1.0 / r[:, 2 * A_HD:4 * A_HD])
        o4 = jnp.concatenate(
            [jnp.where(first_head, r[2 * c * CHUNK:(2 * c + 1) * CHUNK, :],
                       r[(2 * c + 1) * CHUNK:(2 * c + 2) * CHUNK, :]) for c in range(2)], axis=0)
        o_s[rows(g), lanes(j)] = (o4 * sz_ref[rows(g), lanes(j)].astype(F32)).astype(BF16)

    even, odd = (s_even, m_even), (s_odd, m_odd)

    def run(masked):
        for j in range(A_PAIRS):
            scores(masked, 0, j, *even)
        for g in range(n_groups):
            cur, nxt = (even, odd) if g % 2 == 0 else (odd, even)
            for j in range(A_PAIRS):
                finish(g, j, *cur)
                if g + 1 < n_groups:
                    scores(masked, g + 1, j, *nxt)

    @pl.when(i == 0)
    def _():
        run(True)

    @pl.when(i > 0)
    def _():
        run(False)

    out_ref[...] = x_ref[...] + jnp.dot(o_s[...], wout_ref[...], preferred_element_type=F32)


def _attn(q, k, v, sz, x, bias, w_out):
    B, T, D = x.shape
    bm = ATT_BM
    nt = T // bm
    once = lambda *shape: pl.BlockSpec(shape, lambda b, t: (0,) * len(shape),
                                       pipeline_mode=pl.Buffered(1))
    cur = lambda width: pl.BlockSpec((None, bm, width), lambda b, t: (b, t, 0))
    prev = lambda width: pl.BlockSpec((None, bm, width), lambda b, t: (b, jnp.maximum(t - 1, 0), 0))
    return pl.pallas_call(
        _attn_kernel,
        grid=(B, nt),
        in_specs=[cur(A_INNER), prev(A_INNER), cur(A_INNER), prev(A_INNER), cur(A_INNER),
                  cur(A_INNER), cur(D), once(A_PAIRS, 4 * CHUNK, BAND2), once(A_INNER, D)],
        out_specs=cur(D),
        out_shape=jax.ShapeDtypeStruct((B, T, D), F32),
        scratch_shapes=[
            pltpu.VMEM((bm, A_INNER), BF16),
            pltpu.VMEM((A_PAIRS, 4 * CHUNK, BAND2), F32),
            pltpu.VMEM((A_PAIRS, 4 * CHUNK, BAND2), F32),
            pltpu.VMEM((A_PAIRS, 4 * CHUNK, LANES), F32),
            pltpu.VMEM((A_PAIRS, 4 * CHUNK, LANES), F32),
        ],
        compiler_params=pltpu.CompilerParams(
            dimension_semantics=("parallel", "parallel"), vmem_limit_bytes=VMEM_LIMIT),
        name="attn",
    )(q, k, k, v, v, sz, x, bias, w_out)


def kernel(x, norm_g, a_w_in, a_conv_w, a_conv_b, a_w_q, a_w_k, a_w_v, a_w_o, a_w_if, a_b_if,
           a_out_g, a_skip, a_w_out, b_w_in, b_q_g, b_k_g, b_rel_bias, b_w_out):
    assert x.shape[1] % ATT_BM == 0 and x.shape[2] == D_MODEL
    bf = lambda w: w.astype(BF16)
    pad_gate = lambda w: jnp.pad(w, ((0, 0), (0, GATE_PAD - 2 * M_HEADS)))

    x1 = _mlstm_layer(
        x, norm_g[0][None, :], bf(a_w_in[0]), a_conv_w[0], a_conv_b[0][None, :],
        bf(a_w_q[0]), bf(a_w_k[0]), bf(a_w_v[0]), bf(a_w_o[0]),
        bf(pad_gate(a_w_if[0])), pad_gate(a_b_if[0][None, :]), a_skip[0][None, :],
        a_out_g[0].reshape(1, M_INNER), bf(a_w_out[0]))

    bias = _attn_bias(b_rel_bias[0])
    gain = jnp.concatenate([jnp.tile(b_q_g[0], A_HEADS) * (A_HD ** -0.5 * LOG2_E),
                            jnp.tile(b_k_g[0], A_HEADS)])[None, :]
    q, k, v, sz = _attn_pre(x1, norm_g[1][None, :], bf(b_w_in[0]), gain)
    return _attn(q, k, v, sz, x1, bias, bf(b_w_out[0]))
```

```python
import functools

import jax
import jax.numpy as jnp
from jax import lax
from jax.experimental import pallas as pl
from jax.experimental.pallas import tpu as pltpu

F32 = jnp.float32
BF16 = jnp.bfloat16

RMS_EPS = 1e-6
CHUNK = 64
LOG2_E = 1.4426950408889634

D_MODEL = 1024
M_INNER = 2 * D_MODEL
M_HEADS = 4
M_DV = M_INNER // M_HEADS
M_DQK = M_DV // 2
CONV_K = 4

A_INNER = D_MODEL
A_HEADS = 16
A_HD = A_INNER // A_HEADS
A_PAIRS = A_HEADS // 2
LEFT_CHUNKS = 8
BAND = (LEFT_CHUNKS + 1) * CHUNK
REL_PAST = 256
REL_FUTURE = CHUNK - 1
REL_SIZE = REL_PAST + REL_FUTURE + 1

LANES = 128
SUBLANES = 8
GATE_PAD = LANES
VMEM_LIMIT = 56 * 1024 * 1024
FUSED_VMEM_LIMIT = 60 * 1024 * 1024
REC_L = 256
APRE_BM = 1024
ATT_BM = LEFT_CHUNKS * CHUNK
BAND2 = BAND + CHUNK
BIAS_W = BAND2

NT_DIMS = (((1,), (1,)), ((), ()))
TN_DIMS = (((0,), (0,)), ((), ()))


def _sigmoid(x):
    return 0.5 * jnp.tanh(0.5 * x) + 0.5


def _silu(x):
    h = 0.5 * x
    return h + h * jnp.tanh(h)


def _log_sigmoid(x):
    return jnp.minimum(x, 0.0) - jnp.log1p(jnp.exp(-jnp.abs(x)))


def _rms(x, eps=RMS_EPS):
    return x * lax.rsqrt(jnp.mean(x * x, axis=-1, keepdims=True) + eps)


def _mlstm_layer_kernel(xp_ref, xc_ref, g_ref, win_ref, cw_ref, cb_ref, wq_ref, wk_ref, wv_ref, wo_ref,
                        wif_ref, bif_ref, skip_ref, og_ref, wout_ref,
                        out_ref,
                        ubuf, cb_scr, xn_scr, hq, hk, hv, ho, hcs, hsz, hgc, hgr,
                        ctx_ref, ctxb_ref, n_ref, m_ref, s_scr, qc_scr, qn_scr, rs_scr, p_scr, kw_scr,
                        pv_scr, cu_scr, y_scr,
                        *, blocks_per_row, n_blocks):
    step = pl.program_id(0)
    L = xp_ref.shape[0]
    wr = step % 2
    rd = 1 - wr
    prod_t = jnp.minimum(step, n_blocks - 1) % blocks_per_row
    cons_t = jnp.maximum(step - 1, 0) % blocks_per_row
    heads = range(M_HEADS)
    dv = lambda h: slice(h * M_DV, (h + 1) * M_DV)
    qk = lambda h: slice(h * M_DQK, (h + 1) * M_DQK)

    @pl.when(step == 0)
    def _():
        for ref in (hq, hk, hv, ho, hcs, hsz, hgc, hgr):
            ref[1] = jnp.zeros(ref.shape[1:], ref.dtype)

    @pl.when(prod_t == 0)
    def _():
        ubuf[0:SUBLANES, :] = jnp.zeros((SUBLANES, M_INNER), F32)

    @pl.when(cons_t == 0)
    def _():
        ctx_ref[...] = jnp.zeros(ctx_ref.shape, F32)
        ctxb_ref[...] = jnp.zeros(ctxb_ref.shape, BF16)
        n_ref[...] = jnp.zeros(n_ref.shape, F32)
        m_ref[...] = jnp.zeros(m_ref.shape, F32)

    for h in heads:
        qh = hq[rd, :, qk(h)]
        s_scr[h] = lax.dot_general(qh, hk[rd, :, qk(h)], NT_DIMS, preferred_element_type=F32)
        qc_scr[h] = jnp.dot(qh, ctxb_ref[h], preferred_element_type=F32)
        qn_scr[h] = jnp.broadcast_to(
            jnp.sum(qh.astype(F32) * n_ref[h], axis=1, keepdims=True), (L, LANES))

    xn_scr[...] = (_rms(xp_ref[...]) * g_ref[...]).astype(BF16)

    gc = hgc[rd]
    b = gc
    ar = pltpu.roll(gc, GATE_PAD - M_HEADS, 1)
    cm = pltpu.roll(gc, GATE_PAD - 2 * M_HEADS, 1)
    m_prev = m_ref[...]
    g_last = b[L - 1:L, :]
    a_c = g_last + ar
    m_new = jnp.maximum(g_last + m_prev, jnp.max(a_c, axis=0, keepdims=True))
    w_all = jnp.exp(a_c - m_new)
    decay = jnp.exp(g_last + m_prev - m_new)
    mx = jnp.maximum(m_prev, cm)
    inter_all = jnp.exp(m_prev - mx)
    enmt_all = jnp.exp(-(b + mx))
    m_ref[...] = m_new
    gr2 = hgr[rd] * LOG2_E
    mx2 = mx * LOG2_E
    tril = (lax.broadcasted_iota(jnp.int32, (L, L), 1)
            <= lax.broadcasted_iota(jnp.int32, (L, L), 0))

    for h in heads:
        d = jnp.where(tril, jnp.exp2(gr2[h:h + 1, :] - mx2[:, h:h + 1]), 0.0)
        p = s_scr[h] * d
        p_scr[h] = p.astype(BF16)
        rs_scr[h] = jnp.broadcast_to(jnp.sum(p, axis=1, keepdims=True), (L, LANES))
        kw = hk[rd, :, qk(h)].astype(F32) * w_all[:, h:h + 1]
        kw_scr[h] = kw.astype(BF16)
        n_ref[h] = decay[:, h:h + 1] * n_ref[h] + jnp.sum(kw, axis=0, keepdims=True)

    for h in heads:
        ubuf[SUBLANES:SUBLANES + L, dv(h)] = jnp.dot(xn_scr[...], win_ref[:, dv(h)],
                                                     preferred_element_type=F32)

    def numerators_and_increments():
        for h in heads:
            vh = hv[rd, :, dv(h)]
            pv_scr[h] = jnp.dot(p_scr[h], vh, preferred_element_type=F32)
            cu_scr[h] = lax.dot_general(kw_scr[h], vh, TN_DIMS, preferred_element_type=F32)

    def conv_branch(h):
        u = ubuf[SUBLANES:SUBLANES + L, dv(h)]
        conv = cb_ref[:, dv(h)] + cw_ref[CONV_K - 1:CONV_K, dv(h)] * u
        for j in range(CONV_K - 1):
            shifted = ubuf[pl.ds(SUBLANES - (CONV_K - 1) + j, L), dv(h)]
            conv = conv + cw_ref[j:j + 1, dv(h)] * shifted
        c = _silu(conv)
        cb_scr[:, dv(h)] = c.astype(BF16)
        hcs[wr, :, dv(h)] = (skip_ref[:, dv(h)] * c).astype(BF16)

    def value_projections(h):
        ub16 = ubuf[SUBLANES:SUBLANES + L, dv(h)].astype(BF16)
        hv[wr, :, dv(h)] = jnp.dot(ub16, wv_ref[h], preferred_element_type=F32).astype(BF16)
        ho[wr, :, dv(h)] = _sigmoid(jnp.dot(ub16, wo_ref[h], preferred_element_type=F32)).astype(BF16)

    def key_projections(h):
        cb16 = cb_scr[:, dv(h)]
        hq[wr, :, qk(h)] = jnp.dot(cb16, wq_ref[h], preferred_element_type=F32).astype(BF16)
        hk[wr, :, qk(h)] = (jnp.dot(cb16, wk_ref[h], preferred_element_type=F32)
                            * (M_DQK ** -0.5)).astype(BF16)

    def gate_branch(h):
        z = jnp.dot(xn_scr[...], win_ref[:, M_INNER + h * M_DV:M_INNER + (h + 1) * M_DV],
                    preferred_element_type=F32)
        hsz[wr, :, dv(h)] = _silu(z).astype(BF16)

    def normalise(h):
        inter = jnp.broadcast_to(inter_all[:, h:h + 1], (L, LANES))
        enmt = jnp.broadcast_to(enmt_all[:, h:h + 1], (L, LANES))
        num = pv_scr[h] + jnp.tile(inter, (1, M_DV // LANES)) * qc_scr[h]
        den = rs_scr[h] + inter * qn_scr[h]
        r = 1.0 / jnp.maximum(jnp.abs(den), enmt)
        ms = jnp.mean(num * num, axis=1, keepdims=True)
        scale = r * lax.rsqrt(r * r * ms + RMS_EPS)
        hn = (num * jnp.tile(scale, (1, M_DV // LANES)) * og_ref[:, dv(h)]).astype(BF16)
        y_scr[:, dv(h)] = (ho[rd, :, dv(h)] * hn + hcs[rd, :, dv(h)]) * hsz[rd, :, dv(h)]

    def project(h):
        part = jnp.dot(y_scr[:, dv(h)], wout_ref[dv(h), :], preferred_element_type=F32)
        if h == 0:
            out_ref[...] = xc_ref[...] + part
        else:
            out_ref[...] += part

    for h in heads:
        conv_branch(h)
    ubuf[0:SUBLANES, :] = ubuf[L:L + SUBLANES, :]
    numerators_and_increments()

    def gate_algebra():
        gates = bif_ref[...] + jnp.dot(cb_scr[...], wif_ref[...], preferred_element_type=F32)
        lf = _log_sigmoid(gates)
        lf_hi = lf.astype(BF16)
        rem = lf - lf_hi.astype(F32)
        lf_mid = rem.astype(BF16)
        lf_lo = (rem - lf_mid.astype(F32)).astype(BF16)
        trilb = tril.astype(BF16)
        cum = (jnp.dot(trilb, lf_hi, preferred_element_type=F32)
               + jnp.dot(trilb, lf_mid, preferred_element_type=F32)
               + jnp.dot(trilb, lf_lo, preferred_element_type=F32))
        nb = pltpu.roll(cum, GATE_PAD - M_HEADS, 1)
        nar = gates - nb
        row = lax.broadcasted_iota(jnp.int32, (L, GATE_PAD), 0)
        ncm = nar
        shift = 1
        while shift < L:
            ncm = jnp.maximum(ncm, jnp.where(row >= shift, pltpu.roll(ncm, shift, 0), -jnp.inf))
            shift *= 2
        lane = lax.broadcasted_iota(jnp.int32, (L, GATE_PAD), 1)
        hgc[wr] = jnp.where(lane < M_HEADS, nb,
                            jnp.where(lane < 2 * M_HEADS, pltpu.roll(nar, M_HEADS, 1),
                                      pltpu.roll(ncm, 2 * M_HEADS, 1)))
        hgr[wr] = nar.T[0:2 * M_HEADS, :]

    for h in heads:
        if h == M_HEADS // 2:
            gate_algebra()
        gate_branch(h)
        value_projections(h)
        key_projections(h)
        normalise(h)
        if h > 0:
            project(h - 1)
    project(M_HEADS - 1)

    for h in heads:
        ctx = decay[:, h:h + 1] * ctx_ref[h] + cu_scr[h]
        ctx_ref[h] = ctx
        ctxb_ref[h] = ctx.astype(BF16)


def _mlstm_layer(x, g, w_in, conv_w, conv_b, w_q, w_k, w_v, w_o, w_if, b_if, skip, out_g, w_out):
    B, T, D = x.shape
    L = REC_L
    nt = T // L
    n_blocks = B * nt
    once = lambda *shape: pl.BlockSpec(shape, lambda s: (0,) * len(shape), pipeline_mode=pl.Buffered(1))

    def prod_map(s):
        blk = jnp.minimum(s, n_blocks - 1)
        return (blk // nt, blk % nt, 0)

    def cons_map(s):
        blk = jnp.maximum(s - 1, 0)
        return (blk // nt, blk % nt, 0)

    hand = lambda width, dtype: pltpu.VMEM((2, L, width), dtype)
    return pl.pallas_call(
        functools.partial(_mlstm_layer_kernel, blocks_per_row=nt, n_blocks=n_blocks),
        grid=(n_blocks + 1,),
        in_specs=[
            pl.BlockSpec((None, L, D), prod_map), pl.BlockSpec((None, L, D), cons_map),
            once(1, D), once(D, 2 * M_INNER), once(CONV_K, M_INNER), once(1, M_INNER),
            once(M_HEADS, M_DV, M_DQK), once(M_HEADS, M_DV, M_DQK),
            once(M_HEADS, M_DV, M_DV), once(M_HEADS, M_DV, M_DV),
            once(M_INNER, GATE_PAD), once(1, GATE_PAD), once(1, M_INNER),
            once(1, M_INNER), once(M_INNER, D),
        ],
        out_specs=pl.BlockSpec((None, L, D), cons_map),
        out_shape=jax.ShapeDtypeStruct((B, T, D), F32),
        scratch_shapes=[
            pltpu.VMEM((SUBLANES + L, M_INNER), F32),
            pltpu.VMEM((L, M_INNER), BF16),
            pltpu.VMEM((L, D_MODEL), BF16),
            hand(M_HEADS * M_DQK, BF16), hand(M_HEADS * M_DQK, BF16),
            hand(M_INNER, BF16), hand(M_INNER, BF16),
            hand(M_INNER, BF16), hand(M_INNER, BF16),
            hand(GATE_PAD, F32), pltpu.VMEM((2, 2 * M_HEADS, L), F32),
            pltpu.VMEM((M_HEADS, M_DQK, M_DV), F32),
            pltpu.VMEM((M_HEADS, M_DQK, M_DV), BF16),
            pltpu.VMEM((M_HEADS, 1, M_DQK), F32),
            pltpu.VMEM((1, LANES), F32),
            pltpu.VMEM((M_HEADS, L, L), F32),
            pltpu.VMEM((M_HEADS, L, M_DV), F32),
            pltpu.VMEM((M_HEADS, L, LANES), F32),
            pltpu.VMEM((M_HEADS, L, LANES), F32),
            pltpu.VMEM((M_HEADS, L, L), BF16),
            pltpu.VMEM((M_HEADS, L, M_DQK), BF16),
            pltpu.VMEM((M_HEADS, L, M_DV), F32),
            pltpu.VMEM((M_HEADS, M_DQK, M_DV), F32),
            pltpu.VMEM((L, M_INNER), BF16),
        ],
        compiler_params=pltpu.CompilerParams(
            dimension_semantics=("arbitrary",), vmem_limit_bytes=FUSED_VMEM_LIMIT),
        name="mlstm_layer",
    )(x, x, g, w_in, conv_w, conv_b, w_q, w_k, w_v, w_o, w_if, b_if, skip, out_g, w_out)


def _bias_kernel(tab_ref, out_ref):
    r = lax.broadcasted_iota(jnp.int32, (REL_SIZE, BIAS_W), 0)
    e = lax.broadcasted_iota(jnp.int32, (REL_SIZE, BIAS_W), 1)
    e = jnp.where(e >= BAND, e - BIAS_W, e)
    idx = jnp.clip(LEFT_CHUNKS * CHUNK - e, -REL_FUTURE, REL_PAST) + REL_FUTURE
    onehot = (r == idx).astype(F32)
    g = jnp.dot(tab_ref[...], onehot, precision=lax.Precision.HIGHEST,
                preferred_element_type=F32) * LOG2_E
    for j in range(A_PAIRS):
        halves = []
        for a in range(2):
            rows = jnp.broadcast_to(g[2 * j + a:2 * j + a + 1, :], (CHUNK, BIAS_W))
            halves.append(pltpu.roll(rows, 0, 1, stride=1, stride_axis=0))
        both = jnp.concatenate(halves, axis=0)
        col = lax.broadcasted_iota(jnp.int32, (2 * CHUNK, BAND2), 1)
        first = jnp.where(col < BAND, both, -jnp.inf)
        second = jnp.where(col >= CHUNK, pltpu.roll(both, CHUNK, 1), -jnp.inf)
        out_ref[j] = jnp.concatenate([first, second], axis=0)


def _attn_bias(rel_bias):
    return pl.pallas_call(
        _bias_kernel,
        out_shape=jax.ShapeDtypeStruct((A_PAIRS, 4 * CHUNK, BAND2), F32),
        compiler_params=pltpu.CompilerParams(vmem_limit_bytes=VMEM_LIMIT),
        name="attn_bias",
    )(rel_bias)


def _attn_pre_kernel(x_ref, g_ref, win_ref, gain_ref,
                     q_ref, k_ref, v_ref, sz_ref,
                     qk_scr):
    bm = x_ref.shape[0]
    xn = (_rms(x_ref[...]) * g_ref[...]).astype(BF16)
    qk_scr[...] = jnp.dot(xn, win_ref[:, 0:2 * A_INNER], preferred_element_type=F32)
    v_ref[...] = jnp.dot(xn, win_ref[:, 2 * A_INNER:3 * A_INNER],
                         preferred_element_type=F32).astype(BF16)
    z = jnp.dot(xn, win_ref[:, 3 * A_INNER:4 * A_INNER], preferred_element_type=F32)
    sz_ref[...] = _silu(z).astype(BF16)

    first_head = lax.broadcasted_iota(jnp.int32, (bm, 2 * A_HD), 1) < A_HD
    for j in range(2 * A_PAIRS):
        lanes = slice(j * 2 * A_HD, (j + 1) * 2 * A_HD)
        xt = qk_scr[:, lanes]
        sq = xt * xt
        head_a = jnp.sum(jnp.where(first_head, sq, 0.0), axis=1, keepdims=True)
        head_b = jnp.sum(jnp.where(first_head, 0.0, sq), axis=1, keepdims=True)
        inv_a = lax.rsqrt(head_a * (1.0 / A_HD) + RMS_EPS)
        inv_b = lax.rsqrt(head_b * (1.0 / A_HD) + RMS_EPS)
        out = (xt * jnp.where(first_head, inv_a, inv_b) * gain_ref[:, lanes]).astype(BF16)
        if j < A_PAIRS:
            q_ref[:, lanes] = out
        else:
            k_ref[:, j * 2 * A_HD - A_INNER:(j + 1) * 2 * A_HD - A_INNER] = out


def _attn_pre(x, g, w_in, gain):
    B, T, D = x.shape
    bm = min(APRE_BM, T)
    nt = T // bm
    full = lambda *shape: pl.BlockSpec(shape, lambda b, t: (0,) * len(shape))
    tok = lambda width: pl.BlockSpec((None, bm, width), lambda b, t: (b, t, 0))
    o = jax.ShapeDtypeStruct((B, T, A_INNER), BF16)
    return pl.pallas_call(
        _attn_pre_kernel,
        grid=(B, nt),
        in_specs=[tok(D), full(1, D), full(D, 4 * A_INNER), full(1, 2 * A_INNER)],
        out_specs=(tok(A_INNER),) * 4,
        out_shape=(o, o, o, o),
        scratch_shapes=[pltpu.VMEM((bm, 2 * A_INNER), F32)],
        compiler_params=pltpu.CompilerParams(
            dimension_semantics=("parallel", "parallel"), vmem_limit_bytes=VMEM_LIMIT),
        name="attn_pre",
    )(x, g, w_in, gain)


def _attn_kernel(q_ref, kp_ref, kc_ref, vp_ref, vc_ref, sz_ref, x_ref, bias_ref, wout_ref,
                 out_ref,
                 o_s, s_even, s_odd, m_even, m_odd):
    i = pl.program_id(1)
    bm = q_ref.shape[0]
    group = 2 * CHUNK
    n_groups = bm // group
    first_head = lax.broadcasted_iota(jnp.int32, (CHUNK, 2 * A_HD), 1) < A_HD
    key_off = lax.broadcasted_iota(jnp.int32, (1, BAND2), 1)
    ones_tile = jnp.ones((bm, LANES), BF16)

    lanes = lambda j: slice(j * 2 * A_HD, (j + 1) * 2 * A_HD)
    rows = lambda g: slice(g * group, (g + 1) * group)
    old = lambda g: slice(g * group + bm - LEFT_CHUNKS * CHUNK, bm)
    new = lambda g: slice(0, (g + 1) * group)
    n_old = lambda g: LEFT_CHUNKS * CHUNK - g * group

    def scores(masked, g, j, s_ref, m_ref):
        q4 = q_ref[rows(g), lanes(j)].astype(F32)
        parts = []
        for c in range(2):
            qc = q4[c * CHUNK:(c + 1) * CHUNK, :]
            parts += [jnp.where(first_head, qc, 0.0), jnp.where(first_head, 0.0, qc)]
        wt = jnp.concatenate(parts, axis=0).astype(BF16)
        s = jnp.concatenate(
            [lax.dot_general(wt, kp_ref[old(g), lanes(j)], NT_DIMS, preferred_element_type=F32),
             lax.dot_general(wt, kc_ref[new(g), lanes(j)], NT_DIMS, preferred_element_type=F32)],
            axis=1) + bias_ref[j]
        if masked:
            valid = (g * group - LEFT_CHUNKS * CHUNK + key_off) >= 0
            s = jnp.where(valid, s, -jnp.inf)
        s_ref[j] = s
        m_ref[j] = jnp.broadcast_to(jnp.max(s, axis=1, keepdims=True), (2 * group, LANES))

    def finish(g, j, s_ref, m_ref):
        m = m_ref[j]
        e = jnp.exp2(s_ref[j] - jnp.concatenate([m] * (BAND2 // LANES), axis=1)).astype(BF16)
        v_old = jnp.concatenate([vp_ref[old(g), lanes(j)], ones_tile[old(g)]], axis=1)
        v_new = jnp.concatenate([vc_ref[new(g), lanes(j)], ones_tile[new(g)]], axis=1)
        r = (jnp.dot(e[:, 0:n_old(g)], v_old, preferred_element_type=F32)
             + jnp.dot(e[:, n_old(g):BAND2], v_new, preferred_element_type=F32))
        r = r[:, 0:2 * A_HD] * (1.0 / r[:, 2 * A_HD:4 * A_HD])
        o4 = jnp.concatenate(
            [jnp.where(first_head, r[2 * c * CHUNK:(2 * c + 1) * CHUNK, :],
                       r[(2 * c + 1) * CHUNK:(2 * c + 2) * CHUNK, :]) for c in range(2)], axis=0)
        o_s[rows(g), lanes(j)] = (o4 * sz_ref[rows(g), lanes(j)].astype(F32)).astype(BF16)

    even, odd = (s_even, m_even), (s_odd, m_odd)

    def run(masked):
        for j in range(A_PAIRS):
            scores(masked, 0, j, *even)
        for g in range(n_groups):
            cur, nxt = (even, odd) if g % 2 == 0 else (odd, even)
            for j in range(A_PAIRS):
                finish(g, j, *cur)
                if g + 1 < n_groups:
                    scores(masked, g + 1, j, *nxt)

    @pl.when(i == 0)
    def _():
        run(True)

    @pl.when(i > 0)
    def _():
        run(False)

    out_ref[...] = x_ref[...] + jnp.dot(o_s[...], wout_ref[...], preferred_element_type=F32)


def _attn(q, k, v, sz, x, bias, w_out):
    B, T, D = x.shape
    bm = ATT_BM
    nt = T // bm
    once = lambda *shape: pl.BlockSpec(shape, lambda b, t: (0,) * len(shape),
                                       pipeline_mode=pl.Buffered(1))
    cur = lambda width: pl.BlockSpec((None, bm, width), lambda b, t: (b, t, 0))
    prev = lambda width: pl.BlockSpec((None, bm, width), lambda b, t: (b, jnp.maximum(t - 1, 0), 0))
    return pl.pallas_call(
        _attn_kernel,
        grid=(B, nt),
        in_specs=[cur(A_INNER), prev(A_INNER), cur(A_INNER), prev(A_INNER), cur(A_INNER),
                  cur(A_INNER), cur(D), once(A_PAIRS, 4 * CHUNK, BAND2), once(A_INNER, D)],
        out_specs=cur(D),
        out_shape=jax.ShapeDtypeStruct((B, T, D), F32),
        scratch_shapes=[
            pltpu.VMEM((bm, A_INNER), BF16),
            pltpu.VMEM((A_PAIRS, 4 * CHUNK, BAND2), F32),
            pltpu.VMEM((A_PAIRS, 4 * CHUNK, BAND2), F32),
            pltpu.VMEM((A_PAIRS, 4 * CHUNK, LANES), F32),
            pltpu.VMEM((A_PAIRS, 4 * CHUNK, LANES), F32),
        ],
        compiler_params=pltpu.CompilerParams(
            dimension_semantics=("parallel", "parallel"), vmem_limit_bytes=VMEM_LIMIT),
        name="attn",
    )(q, k, k, v, v, sz, x, bias, w_out)


def kernel(x, norm_g, a_w_in, a_conv_w, a_conv_b, a_w_q, a_w_k, a_w_v, a_w_o, a_w_if, a_b_if,
           a_out_g, a_skip, a_w_out, b_w_in, b_q_g, b_k_g, b_rel_bias, b_w_out):
    assert x.shape[1] % ATT_BM == 0 and x.shape[2] == D_MODEL
    bf = lambda w: w.astype(BF16)
    pad_gate = lambda w: jnp.pad(w, ((0, 0), (0, GATE_PAD - 2 * M_HEADS)))

    x1 = _mlstm_layer(
        x, norm_g[0][None, :], bf(a_w_in[0]), a_conv_w[0], a_conv_b[0][None, :],
        bf(a_w_q[0]), bf(a_w_k[0]), bf(a_w_v[0]), bf(a_w_o[0]),
        bf(pad_gate(a_w_if[0])), pad_gate(a_b_if[0][None, :]), a_skip[0][None, :],
        a_out_g[0].reshape(1, M_INNER), bf(a_w_out[0]))

    bias = _attn_bias(b_rel_bias[0])
    gain = jnp.concatenate([jnp.tile(b_q_g[0], A_HEADS) * (A_HD ** -0.5 * LOG2_E),
                            jnp.tile(b_k_g[0], A_HEADS)])[None, :]
    q, k, v, sz = _attn_pre(x1, norm_g[1][None, :], bf(b_w_in[0]), gain)
    return _attn(q, k, v, sz, x1, bias, bf(b_w_out[0]))
```
